```python
import math, functools
import jax, jax.numpy as jnp
from jax import lax
import numpy as np

D_MODEL = 1024
BATCH = 4
SEQ = 4096
DEPTH = 1
DEC_BATCH = 32
DEC_SEQ = 8
PAST_LEN = 8192
PAGE_SIZE = 128

N_HEADS = 4
HEAD_DIM = 64
KEY_DIM = 2 * HEAD_DIM
VALUE_DIM = 2 * HEAD_DIM
Q_COLS = N_HEADS * KEY_DIM
ATT_WIDTH = N_HEADS * VALUE_DIM
SCALE = HEAD_DIM ** -0.5
Q_BLOCK = 128
SGU_WIDTH = 512
SGU_GROUPS = 4
SGU_GROUP_DIM = SGU_WIDTH // SGU_GROUPS
CHUNK = 128
D_FF = 2816
IN_COLS = 2 * Q_COLS + ATT_WIDTH + 2 * SGU_WIDTH + 2 * D_MODEL
RMS_EPS = 1e-6
LN_EPS = 1e-5

kernel_name = "diffattn_gmlp_macaron_hybrid_step"


def _lambda_init(layer):
    return 0.8 - 0.6 * math.exp(-0.3 * layer)


def _rms(x, g):
    xf = x.astype(jnp.float32)
    y = xf * lax.rsqrt(jnp.mean(xf * xf, axis=-1, keepdims=True) + RMS_EPS)
    return (y * g.astype(jnp.float32)).astype(x.dtype)


def _layernorm(x, g, b):
    xf = x.astype(jnp.float32)
    mu = jnp.mean(xf, axis=-1, keepdims=True)
    xc = xf - mu
    y = xc * lax.rsqrt(jnp.mean(xc * xc, axis=-1, keepdims=True) + LN_EPS)
    return (y * g.astype(jnp.float32) + b.astype(jnp.float32)).astype(x.dtype)


def _swiglu(x, w_in, w_out):
    gate, up = jnp.split(x @ w_in, 2, axis=-1)
    return (jax.nn.silu(gate) * up) @ w_out


def _diff_lambda(lq1, lk1, lq2, lk2, lam_init):
    f = lambda a: a.astype(jnp.float32)
    return jnp.exp(jnp.sum(f(lq1) * f(lk1))) - jnp.exp(jnp.sum(f(lq2) * f(lk2))) + lam_init


def _diff_weights(s, lam):
    p = jax.nn.softmax(s, axis=-1)
    return p[:, 0] - lam * p[:, 1]


def _attend_prompt(q, k, v, lam):
    b, s = q.shape[:2]
    nblk = s // Q_BLOCK
    qb = q.reshape(b, nblk, Q_BLOCK, N_HEADS, 2, HEAD_DIM).swapaxes(0, 1)
    kpos = jnp.arange(s)

    def one_block(args):
        qi, i = args
        sc = jnp.einsum('bqhcd,bkhcd->bchqk', qi, k, preferred_element_type=jnp.float32) * SCALE
        qpos = i * Q_BLOCK + jnp.arange(Q_BLOCK)
        mask = kpos[None, :] <= qpos[:, None]
        w = _diff_weights(jnp.where(mask, sc, -jnp.inf), lam)
        return jnp.einsum('bhqk,bkhe->bqhe', w.astype(v.dtype), v)

    o = lax.map(one_block, (qb, jnp.arange(nblk)))
    return o.swapaxes(0, 1).reshape(b, s, N_HEADS, VALUE_DIM)


def _attend_sample(q, k, v, lam, k_past, v_past):
    t = q.shape[1]
    p_len = k_past.shape[1]
    s_past = jnp.einsum('bqhcd,bkhcd->bchqk', q, k_past, preferred_element_type=jnp.float32) * SCALE
    s_new = jnp.einsum('bqhcd,bkhcd->bchqk', q, k, preferred_element_type=jnp.float32) * SCALE
    causal = jnp.tril(jnp.ones((t, t), dtype=bool))
    s_new = jnp.where(causal, s_new, -jnp.inf)
    w = _diff_weights(jnp.concatenate([s_past, s_new], axis=-1), lam).astype(v.dtype)
    return (jnp.einsum('bhqk,bkhe->bqhe', w[..., :p_len], v_past)
            + jnp.einsum('bhqk,bkhe->bqhe', w[..., p_len:], v))


def _spatial_chunks(v, w_s, b_s):
    b, s = v.shape[:2]
    vc = v.reshape(b, s // CHUNK, CHUNK, SGU_GROUPS, SGU_GROUP_DIM)
    w = w_s * jnp.tril(jnp.ones((CHUNK, CHUNK), dtype=w_s.dtype))
    out = jnp.einsum('gts,bcsge->bctge', w, vc) + b_s.T[None, None, :, :, None]
    return out.reshape(b, s, SGU_WIDTH)


def _spatial_open_chunk(v, w_s, b_s):
    b, t = v.shape[:2]
    vc = v.reshape(b, t, SGU_GROUPS, SGU_GROUP_DIM)
    w = w_s[:, :t, :t] * jnp.tril(jnp.ones((t, t), dtype=w_s.dtype))
    out = jnp.einsum('gts,bsge->btge', w, vc) + b_s[:, :t].T[None, :, :, None]
    return out.reshape(b, t, SGU_WIDTH)


def _layer(x, attend, spatial, lam_init, n1, f1i, f1o, nm, w_in, lq1, lk1, lq2, lk2, subln,
           ln_g, ln_b, w_s, b_s, wpa, wpb, w_out, n2, f2i, f2o):
    b, t, _ = x.shape
    h = x + 0.5 * _swiglu(_rms(x, n1), f1i, f1o)
    nrm = _rms(h, nm)
    cuts = [int(c) for c in np.cumsum([Q_COLS, Q_COLS, ATT_WIDTH, SGU_WIDTH, SGU_WIDTH, D_MODEL])]
    q, k, v, u, vg, ga, gb = jnp.split(nrm @ w_in, cuts, axis=-1)
    q = q.reshape(b, t, N_HEADS, 2, HEAD_DIM)
    k = k.reshape(b, t, N_HEADS, 2, HEAD_DIM)
    v = v.reshape(b, t, N_HEADS, VALUE_DIM)
    lam = _diff_lambda(lq1, lk1, lq2, lk2, lam_init)
    a = attend(q, k, v, lam)
    a = (_rms(a, subln) * (1.0 - lam_init)).reshape(b, t, ATT_WIDTH)
    vs = _layernorm(jax.nn.gelu(vg, approximate=False), ln_g, ln_b)
    sgu = jax.nn.gelu(u, approximate=False) * spatial(vs, w_s, b_s)
    m = jax.nn.sigmoid(ga) * (a @ wpa) + jax.nn.sigmoid(gb) * (sgu @ wpb)
    h = h + m @ w_out
    out = h + 0.5 * _swiglu(_rms(h, n2), f2i, f2o)
    return out, k.reshape(b, t, N_HEADS, KEY_DIM), v, vs


def setup_inputs(seed: int = 0) -> dict:
    key = jax.random.key(seed)
    ks = iter(jax.random.split(key, 40))
    nrm = lambda shape, scale: jax.random.normal(next(ks), shape, jnp.float32) * scale
    gain = lambda shape: 1.0 + nrm(shape, 0.02)
    n_pages = PAST_LEN // PAGE_SIZE
    n_used = DEC_BATCH * n_pages
    n_pool = n_used + n_used // 4
    page_table = jax.random.permutation(next(ks), n_pool)[:n_used].reshape(DEC_BATCH, n_pages).astype(jnp.int32)
    L = DEPTH
    return {
        "x_prompt": nrm((BATCH, SEQ, D_MODEL), 1.0),
        "x_sample": nrm((DEC_BATCH, DEC_SEQ, D_MODEL), 1.0),
        "cache_k": nrm((L, n_pool, PAGE_SIZE, N_HEADS, KEY_DIM), 1.0),
        "cache_v": nrm((L, n_pool, PAGE_SIZE, N_HEADS, VALUE_DIM), 1.0),
        "page_table": page_table,
        "norm_ffn1": gain((L, D_MODEL)),
        "ffn1_w_in": nrm((L, D_MODEL, 2 * D_FF), D_MODEL ** -0.5),
        "ffn1_w_out": nrm((L, D_FF, D_MODEL), D_FF ** -0.5),
        "norm_mix": gain((L, D_MODEL)),
        "w_in": nrm((L, D_MODEL, IN_COLS), D_MODEL ** -0.5),
        "lambda_q1": nrm((L, HEAD_DIM), 0.1),
        "lambda_k1": nrm((L, HEAD_DIM), 0.1),
        "lambda_q2": nrm((L, HEAD_DIM), 0.1),
        "lambda_k2": nrm((L, HEAD_DIM), 0.1),
        "subln": gain((L, VALUE_DIM)),
        "sgu_ln_g": gain((L, SGU_WIDTH)),
        "sgu_ln_b": nrm((L, SGU_WIDTH), 0.02),
        "sgu_w": nrm((L, SGU_GROUPS, CHUNK, CHUNK), CHUNK ** -0.5),
        "sgu_b": gain((L, SGU_GROUPS, CHUNK)),
        "w_proj_attn": nrm((L, ATT_WIDTH, D_MODEL), ATT_WIDTH ** -0.5),
        "w_proj_sgu": nrm((L, SGU_WIDTH, D_MODEL), SGU_WIDTH ** -0.5),
        "w_out": nrm((L, D_MODEL, D_MODEL), D_MODEL ** -0.5),
        "norm_ffn2": gain((L, D_MODEL)),
        "ffn2_w_in": nrm((L, D_MODEL, 2 * D_FF), D_MODEL ** -0.5),
        "ffn2_w_out": nrm((L, D_FF, D_MODEL), D_FF ** -0.5),
        "norm_final": gain((D_MODEL,)),
    }


def reference(x_prompt, x_sample, cache_k, cache_v, page_table, norm_ffn1, ffn1_w_in, ffn1_w_out,
              norm_mix, w_in, lambda_q1, lambda_k1, lambda_q2, lambda_k2, subln, sgu_ln_g, sgu_ln_b,
              sgu_w, sgu_b, w_proj_attn, w_proj_sgu, w_out, norm_ffn2, ffn2_w_in, ffn2_w_out, norm_final):
    db = x_sample.shape[0]
    past = page_table.shape[1] * PAGE_SIZE
    seq = x_prompt.shape[1]
    tail_start = ((seq - 1) // CHUNK) * CHUNK
    hp, hs = x_prompt, x_sample
    kp, vp, sp, kd, vd, sd = [], [], [], [], [], []
    for l in range(DEPTH):
        w = (norm_ffn1[l], ffn1_w_in[l], ffn1_w_out[l], norm_mix[l], w_in[l], lambda_q1[l], lambda_k1[l],
             lambda_q2[l], lambda_k2[l], subln[l], sgu_ln_g[l], sgu_ln_b[l], sgu_w[l], sgu_b[l],
             w_proj_attn[l], w_proj_sgu[l], w_out[l], norm_ffn2[l], ffn2_w_in[l], ffn2_w_out[l])
        lam_init = _lambda_init(l)
        hp, k_new, v_new, vs_new = _layer(hp, _attend_prompt, _spatial_chunks, lam_init, *w)
        kp.append(k_new); vp.append(v_new); sp.append(vs_new[:, tail_start:])
        k_past = cache_k[l][page_table].reshape(db, past, N_HEADS, 2, HEAD_DIM)
        v_past = cache_v[l][page_table].reshape(db, past, N_HEADS, VALUE_DIM)
        attend_s = functools.partial(_attend_sample, k_past=k_past, v_past=v_past)
        hs, k_new, v_new, vs_new = _layer(hs, attend_s, _spatial_open_chunk, lam_init, *w)
        kd.append(k_new); vd.append(v_new); sd.append(vs_new)
    y_prompt = _rms(hp, norm_final)
    y_sample = _rms(hs, norm_final)
    return (y_prompt, y_sample, jnp.stack(kp), jnp.stack(vp), jnp.stack(sp),
            jnp.stack(kd), jnp.stack(vd), jnp.stack(sd))
```

```python
import functools
import math

import jax
import jax.numpy as jnp
from jax import lax
from jax.experimental import pallas as pl
from jax.experimental.pallas import tpu as pltpu

D_MODEL = 1024
D_FF = 2816
N_HEADS = 4
HEAD_DIM = 64
KEY_DIM = 2 * HEAD_DIM
VALUE_DIM = 2 * HEAD_DIM
Q_COLS = N_HEADS * KEY_DIM
ATT_WIDTH = N_HEADS * VALUE_DIM
SGU_WIDTH = 512
SGU_GROUPS = 4
SGU_GROUP_DIM = SGU_WIDTH // SGU_GROUPS
CHUNK = 128
PAGE_SIZE = 128
SCALE = HEAD_DIM ** -0.5
RMS_EPS = 1e-6
LN_EPS = 1e-5
LAMBDA_INIT = 0.8 - 0.6 * math.exp(-0.3 * 0)
SQRT_HALF = math.sqrt(0.5)

_Q0, _K0, _V0, _U0, _VG0, _GA0, _GB0, _END = 0, 512, 1024, 1536, 2048, 2560, 3584, 4608

BF16 = jnp.bfloat16
F32 = jnp.float32

V7X_VMEM_LIMIT_BYTES = 60 * 1024 * 1024

TOKEN_TILE = 256
ATTN_Q_TILE = 512
ATTN_KV_TILE = 512
PAGES_PER_STEP = 16
SAMPLE_COLS = 128


def _rms(x, g, eps=RMS_EPS):
    return x * lax.rsqrt(jnp.mean(x * x, axis=-1, keepdims=True) + eps) * g


def _gelu(x):
    return 0.5 * x * (1.0 + lax.erf(x * SQRT_HALF))


def _swiglu_half_step(x, g_ref, w_in_ref, w_out_ref):
    xn = _rms(x, g_ref[...]).astype(BF16)
    gate_up = jnp.dot(xn, w_in_ref[...], preferred_element_type=F32)
    gate = gate_up[:, :D_FF]
    up = gate_up[:, D_FF:]
    act = (gate * jax.nn.sigmoid(gate) * up).astype(BF16)
    return x + 0.5 * jnp.dot(act, w_out_ref[...], preferred_element_type=F32)


def _diff_lambda(lq1_ref, lk1_ref, lq2_ref, lk2_ref):
    s1 = jnp.sum(lq1_ref[...] * lk1_ref[...], axis=-1, keepdims=True)
    s2 = jnp.sum(lq2_ref[...] * lk2_ref[...], axis=-1, keepdims=True)
    return jnp.exp(s1) - jnp.exp(s2) + LAMBDA_INIT


def _sub_norm(o1, o2, lam, subln):
    o = o1 - lam * o2
    return _rms(o, subln) * (1.0 - LAMBDA_INIT)


def _pre_kernel(x_ref, n1_ref, w1_ref, w2_ref, nm_ref, win_ref, lng_ref, lnb_ref,
                h_ref, q_ref, k_ref, v_ref, gu_ref, vs_ref, ga_ref, gb_ref):
    h = _swiglu_half_step(x_ref[...], n1_ref, w1_ref, w2_ref)
    h_ref[...] = h
    nrm = _rms(h, nm_ref[...]).astype(BF16)

    def proj(lo, hi):
        return jnp.dot(nrm, win_ref[:, lo:hi], preferred_element_type=F32)

    q_ref[...] = (proj(_Q0, _K0) * SCALE).astype(BF16)
    k_ref[...] = proj(_K0, _V0)
    v_ref[...] = proj(_V0, _U0)
    gu_ref[...] = _gelu(proj(_U0, _VG0))
    gv = _gelu(proj(_VG0, _GA0))
    mu = jnp.mean(gv, axis=-1, keepdims=True)
    gc = gv - mu
    vs_ref[...] = (gc * lax.rsqrt(jnp.mean(gc * gc, axis=-1, keepdims=True) + LN_EPS)
                   * lng_ref[...] + lnb_ref[...])
    ga_ref[...] = jax.nn.sigmoid(proj(_GA0, _GB0))
    gb_ref[...] = jax.nn.sigmoid(proj(_GB0, _END))


def _resident(shape):
    return pl.BlockSpec(shape, lambda *_: (0,) * len(shape), pipeline_mode=pl.Buffered(1))


def _rows(tm, width):
    return pl.BlockSpec((tm, width), lambda i: (i, 0))


def _pre_stage(x, n1, w1, w2, nm, win, lng, lnb):
    t = x.shape[0]
    tm = TOKEN_TILE
    assert t % tm == 0
    widths = (D_MODEL, Q_COLS, Q_COLS, ATT_WIDTH, SGU_WIDTH, SGU_WIDTH, D_MODEL, D_MODEL)
    dtypes = (F32, BF16, F32, F32, F32, F32, F32, F32)
    return pl.pallas_call(
        _pre_kernel,
        grid=(t // tm,),
        in_specs=[_rows(tm, D_MODEL), _resident(n1.shape), _resident(w1.shape), _resident(w2.shape),
                  _resident(nm.shape), _resident(win.shape), _resident(lng.shape), _resident(lnb.shape)],
        out_specs=[_rows(tm, w) for w in widths],
        out_shape=[jax.ShapeDtypeStruct((t, w), d) for w, d in zip(widths, dtypes)],
        compiler_params=pltpu.CompilerParams(dimension_semantics=("parallel",),
                                             vmem_limit_bytes=V7X_VMEM_LIMIT_BYTES),
        name="pre_stage",
    )(x, n1, w1, w2, nm, win, lng, lnb)


def _attn_prompt_kernel(q_ref, k_ref, v_ref, lq1_ref, lk1_ref, lq2_ref, lk2_ref, subln_ref,
                        o_ref, kb_ref, vb_ref):
    qi = pl.program_id(2)
    tq, tk = ATTN_Q_TILE, ATTN_KV_TILE

    @pl.when(qi == 0)
    def _():
        kb_ref[...] = k_ref[...].astype(BF16)
        vb_ref[...] = v_ref[...].astype(BF16)

    q = q_ref[...]
    lane = lax.broadcasted_iota(jnp.int32, q.shape, 1)
    zero = jnp.zeros_like(q)
    qq = jnp.concatenate([jnp.where(lane < HEAD_DIM, q, zero),
                          jnp.where(lane >= HEAD_DIM, q, zero)], axis=0)

    def step(j, carry, masked):
        m, l, acc = carry
        start = pl.multiple_of(j * tk, tk)
        kj = kb_ref[pl.ds(start, tk), :]
        vj = vb_ref[pl.ds(start, tk), :]
        s = lax.dot_general(qq, kj, (((1,), (1,)), ((), ())), preferred_element_type=F32)
        if masked:
            row = lax.broadcasted_iota(jnp.int32, (2 * tq, tk), 0)
            col = lax.broadcasted_iota(jnp.int32, (2 * tq, tk), 1)
            qrow = jnp.where(row >= tq, row - tq, row)
            s = jnp.where(col <= qrow, s, -jnp.inf)
        m_new = jnp.maximum(m, jnp.max(s, axis=-1, keepdims=True))
        alpha = jnp.exp(m - m_new)
        p = jnp.exp(s - m_new)
        l = alpha * l + jnp.sum(p, axis=-1, keepdims=True)
        acc = alpha * acc + jnp.dot(p.astype(BF16), vj, preferred_element_type=F32)
        return m_new, l, acc

    init = (jnp.full((2 * tq, 1), -jnp.inf, F32), jnp.zeros((2 * tq, 1), F32),
            jnp.zeros((2 * tq, VALUE_DIM), F32))
    carry = lax.fori_loop(0, qi, functools.partial(step, masked=False), init)
    _, l, acc = step(qi, carry, masked=True)
    o = acc / l
    lam = _diff_lambda(lq1_ref, lk1_ref, lq2_ref, lk2_ref)
    o_ref[...] = _sub_norm(o[:tq], o[tq:], lam, subln_ref[...]).astype(BF16)


def _attn_prompt(q, k, v, lq1, lk1, lq2, lk2, subln):
    b, s, _ = q.shape
    tq = ATTN_Q_TILE
    assert ATTN_Q_TILE == ATTN_KV_TILE and s % tq == 0
    small = lambda a: pl.BlockSpec(a.shape, lambda bi, hi, qi: (0, 0))
    return pl.pallas_call(
        _attn_prompt_kernel,
        grid=(b, N_HEADS, s // tq),
        in_specs=[pl.BlockSpec((None, tq, KEY_DIM), lambda bi, hi, qi: (bi, qi, hi)),
                  pl.BlockSpec((None, s, KEY_DIM), lambda bi, hi, qi: (bi, 0, hi)),
                  pl.BlockSpec((None, s, VALUE_DIM), lambda bi, hi, qi: (bi, 0, hi)),
                  small(lq1), small(lk1), small(lq2), small(lk2), small(subln)],
        out_specs=pl.BlockSpec((None, tq, VALUE_DIM), lambda bi, hi, qi: (bi, qi, hi)),
        out_shape=jax.ShapeDtypeStruct((b, s, ATT_WIDTH), BF16),
        scratch_shapes=[pltpu.VMEM((s, KEY_DIM), BF16), pltpu.VMEM((s, VALUE_DIM), BF16)],
        compiler_params=pltpu.CompilerParams(
            dimension_semantics=("parallel", "parallel", "arbitrary"),
            vmem_limit_bytes=V7X_VMEM_LIMIT_BYTES),
        name="attn_prompt",
    )(q, k, v, lq1, lk1, lq2, lk2, subln)


def _attn_sample_kernel(pt_ref, q_ref, kn_ref, vn_ref, lq1_ref, lk1_ref, lq2_ref, lk2_ref,
                        subln_ref, *rest):
    del pt_ref
    npg = PAGES_PER_STEP
    k_pages, v_pages = rest[:npg], rest[npg:2 * npg]
    o_ref, m_ref, l_ref, acc_ref, kb_ref, vb_ref = rest[2 * npg:]
    c = pl.program_id(1)
    nt = q_ref.shape[0]
    ncol = SAMPLE_COLS

    q = q_ref[...].astype(F32)
    reps = ncol // nt
    qrows = jnp.concatenate([q] * reps, axis=0)
    rr = lax.broadcasted_iota(jnp.int32, qrows.shape, 1)
    cc = lax.broadcasted_iota(jnp.int32, qrows.shape, 0)
    live = (rr // HEAD_DIM == cc // nt) & (cc < 2 * nt * N_HEADS)
    qm_f32 = jnp.where(live, qrows, 0.0)
    qm = qm_f32.astype(BF16)
    contract_last = (((1,), (1,)), ((), ()))
    contract_first = (((0,), (0,)), ((), ()))

    @pl.when(c == 0)
    def _():
        s = lax.dot_general(kn_ref[...], qm_f32, contract_last, preferred_element_type=F32)
        key_t = lax.broadcasted_iota(jnp.int32, s.shape, 0)
        qry_t = lax.broadcasted_iota(jnp.int32, s.shape, 1) % nt
        s = jnp.where(key_t <= qry_t, s, -jnp.inf)
        m = jnp.max(s, axis=0, keepdims=True)
        p = jnp.exp(s - m)
        m_ref[...] = m
        l_ref[...] = jnp.sum(p, axis=0, keepdims=True)
        acc_ref[...] = lax.dot_general(vn_ref[...], p, contract_first, preferred_element_type=F32)

    for i in range(npg):
        kb_ref[i * PAGE_SIZE:(i + 1) * PAGE_SIZE, :] = k_pages[i][...].astype(BF16)
        vb_ref[i * PAGE_SIZE:(i + 1) * PAGE_SIZE, :] = v_pages[i][...].astype(BF16)

    s = lax.dot_general(kb_ref[...], qm, contract_last, preferred_element_type=F32)
    m_old = m_ref[...]
    m_new = jnp.maximum(m_old, jnp.max(s, axis=0, keepdims=True))
    alpha = jnp.exp(m_old - m_new)
    p = jnp.exp(s - m_new)
    m_ref[...] = m_new
    l_ref[...] = alpha * l_ref[...] + jnp.sum(p, axis=0, keepdims=True)
    acc_ref[...] = alpha * acc_ref[...] + lax.dot_general(
        vb_ref[...], p.astype(BF16), contract_first, preferred_element_type=F32)

    @pl.when(c == pl.num_programs(1) - 1)
    def _():
        o = (acc_ref[...] / l_ref[...]).T
        lam = _diff_lambda(lq1_ref, lk1_ref, lq2_ref, lk2_ref)
        for h in range(N_HEADS):
            r0 = h * 2 * nt
            o1 = o[r0:r0 + nt, h * VALUE_DIM:(h + 1) * VALUE_DIM]
            o2 = o[r0 + nt:r0 + 2 * nt, h * VALUE_DIM:(h + 1) * VALUE_DIM]
            o_ref[:, h * VALUE_DIM:(h + 1) * VALUE_DIM] = _sub_norm(
                o1, o2, lam, subln_ref[...])


def _attn_sample(q, k_new, v_new, cache_k, cache_v, page_table, lq1, lk1, lq2, lk2, subln):
    db, nt, _ = q.shape
    n_pages = page_table.shape[1]
    npg = PAGES_PER_STEP
    assert n_pages % npg == 0 and SAMPLE_COLS % nt == 0 and 2 * nt * N_HEADS <= SAMPLE_COLS
    small = lambda a: pl.BlockSpec(a.shape, lambda b, c, pt: (0, 0))
    per_seq = lambda w: pl.BlockSpec((None, nt, w), lambda b, c, pt: (b, 0, 0))

    def page(i, w):
        return pl.BlockSpec((None, PAGE_SIZE, w), lambda b, c, pt: (pt[b, c * npg + i], 0, 0))

    grid_spec = pltpu.PrefetchScalarGridSpec(
        num_scalar_prefetch=1,
        grid=(db, n_pages // npg),
        in_specs=([per_seq(Q_COLS), per_seq(Q_COLS), per_seq(ATT_WIDTH),
                   small(lq1), small(lk1), small(lq2), small(lk2), small(subln)]
                  + [page(i, Q_COLS) for i in range(npg)]
                  + [page(i, ATT_WIDTH) for i in range(npg)]),
        out_specs=per_seq(ATT_WIDTH),
        scratch_shapes=[pltpu.VMEM((1, SAMPLE_COLS), F32), pltpu.VMEM((1, SAMPLE_COLS), F32),
                        pltpu.VMEM((ATT_WIDTH, SAMPLE_COLS), F32),
                        pltpu.VMEM((npg * PAGE_SIZE, Q_COLS), BF16),
                        pltpu.VMEM((npg * PAGE_SIZE, ATT_WIDTH), BF16)],
    )
    return pl.pallas_call(
        _attn_sample_kernel,
        grid_spec=grid_spec,
        out_shape=jax.ShapeDtypeStruct((db, nt, ATT_WIDTH), F32),
        compiler_params=pltpu.CompilerParams(dimension_semantics=("parallel", "arbitrary"),
                                             vmem_limit_bytes=V7X_VMEM_LIMIT_BYTES),
        name="attn_sample",
    )(page_table, q, k_new, v_new, lq1, lk1, lq2, lk2, subln,
      *([cache_k] * npg), *([cache_v] * npg))


def _post_kernel(h_ref, a_ref, gu_ref, vs_ref, ga_ref, gb_ref, wsp_ref, bsp_ref, wpa_ref, wpb_ref,
                 wo_ref, n2_ref, w1_ref, w2_ref, nf_ref, y_ref, *, seq_per_chunk):
    tm = h_ref.shape[0]
    row = lax.broadcasted_iota(jnp.int32, (CHUNK, CHUNK), 0)
    col = lax.broadcasted_iota(jnp.int32, (CHUNK, CHUNK), 1)
    seq_len = CHUNK // seq_per_chunk
    keep = (col <= row) & (row // seq_len == col // seq_len)
    bias = bsp_ref[...]
    mixed = []
    for ci in range(tm // CHUNK):
        groups = []
        for g in range(SGU_GROUPS):
            w = jnp.where(keep, wsp_ref[g], 0.0).astype(BF16)
            vg = vs_ref[ci * CHUNK:(ci + 1) * CHUNK,
                        g * SGU_GROUP_DIM:(g + 1) * SGU_GROUP_DIM].astype(BF16)
            groups.append(jnp.dot(w, vg, preferred_element_type=F32) + bias[:, g:g + 1])
        mixed.append(jnp.concatenate(groups, axis=1))
    sp = jnp.concatenate(mixed, axis=0) if len(mixed) > 1 else mixed[0]
    sgu = (gu_ref[...] * sp).astype(BF16)
    m = (ga_ref[...] * jnp.dot(a_ref[...].astype(BF16), wpa_ref[...], preferred_element_type=F32)
         + gb_ref[...] * jnp.dot(sgu, wpb_ref[...], preferred_element_type=F32))
    h = h_ref[...] + jnp.dot(m.astype(BF16), wo_ref[...], preferred_element_type=F32)
    out = _swiglu_half_step(h, n2_ref, w1_ref, w2_ref)
    y_ref[...] = _rms(out, nf_ref[...])


def _post_stage(h, a, gu, vs, ga, gb, wsp, bsp, wpa, wpb, wo, n2, w1, w2, nf, *, seq_per_chunk):
    t = h.shape[0]
    tm = TOKEN_TILE
    assert t % tm == 0 and tm % CHUNK == 0
    widths = (D_MODEL, ATT_WIDTH, SGU_WIDTH, SGU_WIDTH, D_MODEL, D_MODEL)
    consts = (wsp, bsp, wpa, wpb, wo, n2, w1, w2, nf)
    return pl.pallas_call(
        functools.partial(_post_kernel, seq_per_chunk=seq_per_chunk),
        grid=(t // tm,),
        in_specs=[_rows(tm, w) for w in widths] + [_resident(c.shape) for c in consts],
        out_specs=_rows(tm, D_MODEL),
        out_shape=jax.ShapeDtypeStruct((t, D_MODEL), F32),
        compiler_params=pltpu.CompilerParams(dimension_semantics=("parallel",),
                                             vmem_limit_bytes=V7X_VMEM_LIMIT_BYTES),
        name="post_stage",
    )(h, a, gu, vs, ga, gb, *consts)


def kernel(x_prompt, x_sample, cache_k, cache_v, page_table, norm_ffn1, ffn1_w_in, ffn1_w_out,
           norm_mix, w_in, lambda_q1, lambda_k1, lambda_q2, lambda_k2, subln, sgu_ln_g, sgu_ln_b,
           sgu_w, sgu_b, w_proj_attn, w_proj_sgu, w_out, norm_ffn2, ffn2_w_in, ffn2_w_out,
           norm_final):
    assert norm_ffn1.shape[0] == 1, "single-layer stack"
    b, s, d = x_prompt.shape
    db, nt, _ = x_sample.shape
    assert CHUNK % nt == 0 and (db * nt) % CHUNK == 0
    tail_start = ((s - 1) // CHUNK) * CHUNK

    bf = lambda w: w[0].astype(BF16)
    pre_w = (norm_ffn1, bf(ffn1_w_in), bf(ffn1_w_out), norm_mix, bf(w_in), sgu_ln_g, sgu_ln_b)
    lam_w = (lambda_q1, lambda_k1, lambda_q2, lambda_k2, subln)
    post_w = (bf(w_proj_attn), bf(w_proj_sgu), bf(w_out), norm_ffn2, bf(ffn2_w_in), bf(ffn2_w_out),
              norm_final.reshape(1, d))

    h, q, k, v, gu, vs, ga, gb = _pre_stage(x_prompt.reshape(b * s, d), *pre_w)
    a = _attn_prompt(q.reshape(b, s, Q_COLS), k.reshape(b, s, Q_COLS), v.reshape(b, s, ATT_WIDTH),
                     *lam_w)
    y_prompt = _post_stage(h, a.reshape(b * s, ATT_WIDTH), gu, vs, ga, gb,
                           sgu_w[0], sgu_b[0].T, *post_w, seq_per_chunk=1)
    k_prompt = k.reshape(1, b, s, N_HEADS, KEY_DIM)
    v_prompt = v.reshape(1, b, s, N_HEADS, VALUE_DIM)
    sgu_v_prompt = vs.reshape(b, s, SGU_WIDTH)[None, :, tail_start:]

    reps = CHUNK // nt
    wsp_s = jnp.tile(sgu_w[0][:, :nt, :nt], (1, reps, reps))
    bsp_s = jnp.tile(sgu_b[0][:, :nt], (1, reps)).T
    hs, qs, ks, vsm, gus, vss, gas, gbs = _pre_stage(x_sample.reshape(db * nt, d), *pre_w)
    n_pool = cache_k.shape[1]
    a_s = _attn_sample(qs.reshape(db, nt, Q_COLS), ks.reshape(db, nt, Q_COLS),
                       vsm.reshape(db, nt, ATT_WIDTH),
                       cache_k[0].reshape(n_pool, PAGE_SIZE, Q_COLS),
                       cache_v[0].reshape(n_pool, PAGE_SIZE, ATT_WIDTH), page_table, *lam_w)
    y_sample = _post_stage(hs, a_s.reshape(db * nt, ATT_WIDTH), gus, vss, gas, gbs,
                           wsp_s, bsp_s, *post_w, seq_per_chunk=reps)
    k_sample = ks.reshape(1, db, nt, N_HEADS, KEY_DIM)
    v_sample = vsm.reshape(1, db, nt, N_HEADS, VALUE_DIM)
    sgu_v_sample = vss.reshape(1, db, nt, SGU_WIDTH)

    return (y_prompt.reshape(b, s, d), y_sample.reshape(db, nt, d), k_prompt, v_prompt,
            sgu_v_prompt, k_sample, v_sample, sgu_v_sample)
```

```python
import functools
import math

import jax
import jax.numpy as jnp
from jax import lax
from jax.experimental import pallas as pl
from jax.experimental.pallas import tpu as pltpu

D_MODEL = 1024
D_FF = 2816
N_HEADS = 4
HEAD_DIM = 64
KEY_DIM = 2 * HEAD_DIM
VALUE_DIM = 2 * HEAD_DIM
Q_COLS = N_HEADS * KEY_DIM
ATT_WIDTH = N_HEADS * VALUE_DIM
SGU_WIDTH = 512
SGU_GROUPS = 4
SGU_GROUP_DIM = SGU_WIDTH // SGU_GROUPS
CHUNK = 128
PAGE_SIZE = 128
SCALE = HEAD_DIM ** -0.5
RMS_EPS = 1e-6
LN_EPS = 1e-5
LAMBDA_INIT = 0.8 - 0.6 * math.exp(-0.3 * 0)
SQRT_HALF = math.sqrt(0.5)

_Q0, _K0, _V0, _U0, _VG0, _GA0, _GB0, _END = 0, 512, 1024, 1536, 2048, 2560, 3584, 4608

BF16 = jnp.bfloat16
F32 = jnp.float32

V7X_VMEM_LIMIT_BYTES = 60 * 1024 * 1024

TOKEN_TILE = 256
ATTN_Q_TILE = 512
ATTN_KV_TILE = 512
PAGES_PER_STEP = 16
SAMPLE_COLS = 128


def _rms(x, g, eps=RMS_EPS):
    return x * lax.rsqrt(jnp.mean(x * x, axis=-1, keepdims=True) + eps) * g


def _gelu(x):
    return 0.5 * x * (1.0 + lax.erf(x * SQRT_HALF))


def _swiglu_half_step(x, g_ref, w_in_ref, w_out_ref):
    xn = _rms(x, g_ref[...]).astype(BF16)
    gate_up = jnp.dot(xn, w_in_ref[...], preferred_element_type=F32)
    gate = gate_up[:, :D_FF]
    up = gate_up[:, D_FF:]
    act = (gate * jax.nn.sigmoid(gate) * up).astype(BF16)
    return x + 0.5 * jnp.dot(act, w_out_ref[...], preferred_element_type=F32)


def _diff_lambda(lq1_ref, lk1_ref, lq2_ref, lk2_ref):
    s1 = jnp.sum(lq1_ref[...] * lk1_ref[...], axis=-1, keepdims=True)
    s2 = jnp.sum(lq2_ref[...] * lk2_ref[...], axis=-1, keepdims=True)
    return jnp.exp(s1) - jnp.exp(s2) + LAMBDA_INIT


def _sub_norm(o1, o2, lam, subln):
    o = o1 - lam * o2
    return _rms(o, subln) * (1.0 - LAMBDA_INIT)


def _pre_kernel(x_ref, n1_ref, w1_ref, w2_ref, nm_ref, win_ref, lng_ref, lnb_ref,
                h_ref, k_ref, v_ref, gu_ref, vs_ref, ga_ref, gb_ref, *attn_refs):
    h = _swiglu_half_step(x_ref[...], n1_ref, w1_ref, w2_ref)
    h_ref[...] = h
    nrm = _rms(h, nm_ref[...]).astype(BF16)

    def proj(lo, hi):
        return jnp.dot(nrm, win_ref[:, lo:hi], preferred_element_type=F32)

    q = proj(_Q0, _K0) * SCALE
    k = proj(_K0, _V0)
    v = proj(_V0, _U0)
    head = lambda z, hd: z[:, hd * KEY_DIM:(hd + 1) * KEY_DIM]
    tm = x_ref.shape[0]
    for hd in range(N_HEADS):
        k_ref[pl.ds(hd, tm, stride=N_HEADS), :] = head(k, hd)
        v_ref[pl.ds(hd, tm, stride=N_HEADS), :] = head(v, hd)
    if len(attn_refs) == 3:
        for z, ref in zip((q, k, v), attn_refs):
            for hd in range(N_HEADS):
                ref[hd] = head(z, hd).astype(BF16)
    else:
        attn_refs[0][...] = q
    gu_ref[...] = _gelu(proj(_U0, _VG0))
    gv = _gelu(proj(_VG0, _GA0))
    mu = jnp.mean(gv, axis=-1, keepdims=True)
    gc = gv - mu
    vs_ref[...] = (gc * lax.rsqrt(jnp.mean(gc * gc, axis=-1, keepdims=True) + LN_EPS)
                   * lng_ref[...] + lnb_ref[...])
    ga_ref[...] = jax.nn.sigmoid(proj(_GA0, _GB0))
    gb_ref[...] = jax.nn.sigmoid(proj(_GB0, _END))


def _resident(shape):
    return pl.BlockSpec(shape, lambda *_: (0,) * len(shape), pipeline_mode=pl.Buffered(1))


def _rows(tm, width):
    return pl.BlockSpec((tm, width), lambda i: (i, 0))


def _pre_stage(x, n1, w1, w2, nm, win, lng, lnb, *, head_major_seq=None):
    t = x.shape[0]
    tm = TOKEN_TILE
    assert t % tm == 0
    cache_rows = _rows(tm * N_HEADS, KEY_DIM)
    cache_shape = jax.ShapeDtypeStruct((t * N_HEADS, KEY_DIM), F32)
    flat = lambda w: (_rows(tm, w), jax.ShapeDtypeStruct((t, w), F32))
    outs = [flat(D_MODEL), (cache_rows, cache_shape), (cache_rows, cache_shape), flat(SGU_WIDTH),
            flat(SGU_WIDTH), flat(D_MODEL), flat(D_MODEL)]
    if head_major_seq is not None:
        seq = head_major_seq
        assert seq % tm == 0 and t % seq == 0
        per = seq // tm
        spec = pl.BlockSpec((None, N_HEADS, tm, KEY_DIM), lambda i: (i // per, 0, i % per, 0))
        outs += [(spec, jax.ShapeDtypeStruct((t // seq, N_HEADS, seq, KEY_DIM), BF16))] * 3
    else:
        outs += [flat(Q_COLS)]
    return pl.pallas_call(
        _pre_kernel,
        grid=(t // tm,),
        in_specs=[_rows(tm, D_MODEL), _resident(n1.shape), _resident(w1.shape), _resident(w2.shape),
                  _resident(nm.shape), _resident(win.shape), _resident(lng.shape), _resident(lnb.shape)],
        out_specs=[o[0] for o in outs],
        out_shape=[o[1] for o in outs],
        compiler_params=pltpu.CompilerParams(dimension_semantics=("parallel",),
                                             vmem_limit_bytes=V7X_VMEM_LIMIT_BYTES),
        name="pre_stage",
    )(x, n1, w1, w2, nm, win, lng, lnb)


def _attn_prompt_kernel(q_ref, kb_ref, vb_ref, lq1_ref, lk1_ref, lq2_ref, lk2_ref, subln_ref,
                        o_ref):
    qi = pl.program_id(2)
    tq, tk = ATTN_Q_TILE, ATTN_KV_TILE

    q = q_ref[...]
    lane = lax.broadcasted_iota(jnp.int32, q.shape, 1)
    zero = jnp.zeros_like(q)
    qq = jnp.concatenate([jnp.where(lane < HEAD_DIM, q, zero),
                          jnp.where(lane >= HEAD_DIM, q, zero)], axis=0)

    def step(j, carry, masked):
        m, l, acc = carry
        start = pl.multiple_of(j * tk, tk)
        kj = kb_ref[pl.ds(start, tk), :]
        vj = vb_ref[pl.ds(start, tk), :]
        s = lax.dot_general(qq, kj, (((1,), (1,)), ((), ())), preferred_element_type=F32)
        if masked:
            row = lax.broadcasted_iota(jnp.int32, (2 * tq, tk), 0)
            col = lax.broadcasted_iota(jnp.int32, (2 * tq, tk), 1)
            qrow = jnp.where(row >= tq, row - tq, row)
            s = jnp.where(col <= qrow, s, -jnp.inf)
        m_new = jnp.maximum(m, jnp.max(s, axis=-1, keepdims=True))
        alpha = jnp.exp(m - m_new)
        p = jnp.exp(s - m_new)
        l = alpha * l + jnp.sum(p, axis=-1, keepdims=True)
        acc = alpha * acc + jnp.dot(p.astype(BF16), vj, preferred_element_type=F32)
        return m_new, l, acc

    init = (jnp.full((2 * tq, 1), -jnp.inf, F32), jnp.zeros((2 * tq, 1), F32),
            jnp.zeros((2 * tq, VALUE_DIM), F32))
    carry = lax.fori_loop(0, qi, functools.partial(step, masked=False), init)
    _, l, acc = step(qi, carry, masked=True)
    o = acc / l
    lam = _diff_lambda(lq1_ref, lk1_ref, lq2_ref, lk2_ref)
    o_ref[...] = _sub_norm(o[:tq], o[tq:], lam, subln_ref[...]).astype(BF16)


def _attn_prompt(q, k, v, lq1, lk1, lq2, lk2, subln):
    b, _, s, _ = q.shape
    tq = ATTN_Q_TILE
    assert ATTN_Q_TILE == ATTN_KV_TILE and s % tq == 0
    small = lambda a: pl.BlockSpec(a.shape, lambda bi, hi, qi: (0, 0))
    whole_seq = pl.BlockSpec((None, None, s, KEY_DIM), lambda bi, hi, qi: (bi, hi, 0, 0))
    return pl.pallas_call(
        _attn_prompt_kernel,
        grid=(b, N_HEADS, s // tq),
        in_specs=[pl.BlockSpec((None, None, tq, KEY_DIM), lambda bi, hi, qi: (bi, hi, qi, 0)),
                  whole_seq, whole_seq,
                  small(lq1), small(lk1), small(lq2), small(lk2), small(subln)],
        out_specs=pl.BlockSpec((None, tq, VALUE_DIM), lambda bi, hi, qi: (bi, qi, hi)),
        out_shape=jax.ShapeDtypeStruct((b, s, ATT_WIDTH), BF16),
        compiler_params=pltpu.CompilerParams(
            dimension_semantics=("parallel", "parallel", "parallel"),
            vmem_limit_bytes=V7X_VMEM_LIMIT_BYTES),
        name="attn_prompt",
    )(q, k, v, lq1, lk1, lq2, lk2, subln)


def _attn_sample_kernel(pt_ref, q_ref, kn_ref, vn_ref, lq1_ref, lk1_ref, lq2_ref, lk2_ref,
                        subln_ref, *rest):
    del pt_ref
    npg = PAGES_PER_STEP
    k_pages, v_pages = rest[:npg], rest[npg:2 * npg]
    o_ref, m_ref, l_ref, acc_ref, kb_ref, vb_ref = rest[2 * npg:]
    c = pl.program_id(1)
    nt = q_ref.shape[0]
    ncol = SAMPLE_COLS

    q = q_ref[...]
    reps = ncol // nt
    qrows = jnp.concatenate([q] * reps, axis=0)
    rr = lax.broadcasted_iota(jnp.int32, qrows.shape, 1)
    cc = lax.broadcasted_iota(jnp.int32, qrows.shape, 0)
    live = (rr // HEAD_DIM == cc // nt) & (cc < 2 * nt * N_HEADS)
    qm_f32 = jnp.where(live, qrows, 0.0)
    qm = qm_f32.astype(BF16)
    contract_last = (((1,), (1,)), ((), ()))
    contract_first = (((0,), (0,)), ((), ()))

    @pl.when(c == 0)
    def _():
        head_rows = lambda ref, hd: ref[pl.ds(hd, nt, stride=N_HEADS), :]
        kn = jnp.concatenate([head_rows(kn_ref, hd) for hd in range(N_HEADS)], axis=1)
        vn = jnp.concatenate([head_rows(vn_ref, hd) for hd in range(N_HEADS)], axis=1)
        s = lax.dot_general(kn, qm_f32, contract_last, preferred_element_type=F32)
        key_t = lax.broadcasted_iota(jnp.int32, s.shape, 0)
        qry_t = lax.broadcasted_iota(jnp.int32, s.shape, 1) % nt
        s = jnp.where(key_t <= qry_t, s, -jnp.inf)
        m = jnp.max(s, axis=0, keepdims=True)
        p = jnp.exp(s - m)
        m_ref[...] = m
        l_ref[...] = jnp.sum(p, axis=0, keepdims=True)
        acc_ref[...] = lax.dot_general(vn, p, contract_first, preferred_element_type=F32)

    for i in range(npg):
        rows = slice(i * PAGE_SIZE, (i + 1) * PAGE_SIZE)
        for hd in range(N_HEADS):
            lanes = slice(hd * KEY_DIM, (hd + 1) * KEY_DIM)
            tokens = pl.ds(hd, PAGE_SIZE, stride=N_HEADS)
            kb_ref[rows, lanes] = k_pages[i][tokens, :].astype(BF16)
            vb_ref[rows, lanes] = v_pages[i][tokens, :].astype(BF16)

    s = lax.dot_general(kb_ref[...], qm, contract_last, preferred_element_type=F32)
    m_old = m_ref[...]
    m_new = jnp.maximum(m_old, jnp.max(s, axis=0, keepdims=True))
    alpha = jnp.exp(m_old - m_new)
    p = jnp.exp(s - m_new)
    m_ref[...] = m_new
    l_ref[...] = alpha * l_ref[...] + jnp.sum(p, axis=0, keepdims=True)
    acc_ref[...] = alpha * acc_ref[...] + lax.dot_general(
        vb_ref[...], p.astype(BF16), contract_first, preferred_element_type=F32)

    @pl.when(c == pl.num_programs(1) - 1)
    def _():
        o = (acc_ref[...] / l_ref[...]).T
        lam = _diff_lambda(lq1_ref, lk1_ref, lq2_ref, lk2_ref)
        for h in range(N_HEADS):
            r0 = h * 2 * nt
            o1 = o[r0:r0 + nt, h * VALUE_DIM:(h + 1) * VALUE_DIM]
            o2 = o[r0 + nt:r0 + 2 * nt, h * VALUE_DIM:(h + 1) * VALUE_DIM]
            o_ref[:, h * VALUE_DIM:(h + 1) * VALUE_DIM] = _sub_norm(
                o1, o2, lam, subln_ref[...])


def _attn_sample(q, k_new, v_new, cache_k, cache_v, page_table, lq1, lk1, lq2, lk2, subln):
    db, nt, _ = q.shape
    n_pages = page_table.shape[1]
    npg = PAGES_PER_STEP
    assert n_pages % npg == 0 and SAMPLE_COLS % nt == 0 and 2 * nt * N_HEADS <= SAMPLE_COLS
    small = lambda a: pl.BlockSpec(a.shape, lambda b, c, pt: (0, 0))
    per_seq = lambda w: pl.BlockSpec((None, nt, w), lambda b, c, pt: (b, 0, 0))

    new_rows = pl.BlockSpec((nt * N_HEADS, KEY_DIM), lambda b, c, pt: (b, 0))

    def page(i):
        return pl.BlockSpec((None, PAGE_SIZE * N_HEADS, KEY_DIM),
                            lambda b, c, pt: (pt[b, c * npg + i], 0, 0))

    grid_spec = pltpu.PrefetchScalarGridSpec(
        num_scalar_prefetch=1,
        grid=(db, n_pages // npg),
        in_specs=([per_seq(Q_COLS), new_rows, new_rows,
                   small(lq1), small(lk1), small(lq2), small(lk2), small(subln)]
                  + [page(i) for i in range(npg)] * 2),
        out_specs=per_seq(ATT_WIDTH),
        scratch_shapes=[pltpu.VMEM((1, SAMPLE_COLS), F32), pltpu.VMEM((1, SAMPLE_COLS), F32),
                        pltpu.VMEM((ATT_WIDTH, SAMPLE_COLS), F32),
                        pltpu.VMEM((npg * PAGE_SIZE, Q_COLS), BF16),
                        pltpu.VMEM((npg * PAGE_SIZE, ATT_WIDTH), BF16)],
    )
    return pl.pallas_call(
        _attn_sample_kernel,
        grid_spec=grid_spec,
        out_shape=jax.ShapeDtypeStruct((db, nt, ATT_WIDTH), F32),
        compiler_params=pltpu.CompilerParams(dimension_semantics=("parallel", "arbitrary"),
                                             vmem_limit_bytes=V7X_VMEM_LIMIT_BYTES),
        name="attn_sample",
    )(page_table, q, k_new, v_new, lq1, lk1, lq2, lk2, subln,
      *([cache_k] * npg), *([cache_v] * npg))


def _post_kernel(h_ref, a_ref, gu_ref, vs_ref, ga_ref, gb_ref, wsp_ref, bsp_ref, wpa_ref, wpb_ref,
                 wo_ref, n2_ref, w1_ref, w2_ref, nf_ref, y_ref, *, seq_per_chunk):
    tm = h_ref.shape[0]
    row = lax.broadcasted_iota(jnp.int32, (CHUNK, CHUNK), 0)
    col = lax.broadcasted_iota(jnp.int32, (CHUNK, CHUNK), 1)
    seq_len = CHUNK // seq_per_chunk
    keep = (col <= row) & (row // seq_len == col // seq_len)
    bias = bsp_ref[...]
    mixed = []
    for ci in range(tm // CHUNK):
        groups = []
        for g in range(SGU_GROUPS):
            w = jnp.where(keep, wsp_ref[g], 0.0).astype(BF16)
            vg = vs_ref[ci * CHUNK:(ci + 1) * CHUNK,
                        g * SGU_GROUP_DIM:(g + 1) * SGU_GROUP_DIM].astype(BF16)
            groups.append(jnp.dot(w, vg, preferred_element_type=F32) + bias[:, g:g + 1])
        mixed.append(jnp.concatenate(groups, axis=1))
    sp = jnp.concatenate(mixed, axis=0) if len(mixed) > 1 else mixed[0]
    sgu = (gu_ref[...] * sp).astype(BF16)
    m = (ga_ref[...] * jnp.dot(a_ref[...].astype(BF16), wpa_ref[...], preferred_element_type=F32)
         + gb_ref[...] * jnp.dot(sgu, wpb_ref[...], preferred_element_type=F32))
    h = h_ref[...] + jnp.dot(m.astype(BF16), wo_ref[...], preferred_element_type=F32)
    out = _swiglu_half_step(h, n2_ref, w1_ref, w2_ref)
    y_ref[...] = _rms(out, nf_ref[...])


def _post_stage(h, a, gu, vs, ga, gb, wsp, bsp, wpa, wpb, wo, n2, w1, w2, nf, *, seq_per_chunk):
    t = h.shape[0]
    tm = TOKEN_TILE
    assert t % tm == 0 and tm % CHUNK == 0
    widths = (D_MODEL, ATT_WIDTH, SGU_WIDTH, SGU_WIDTH, D_MODEL, D_MODEL)
    consts = (wsp, bsp, wpa, wpb, wo, n2, w1, w2, nf)
    return pl.pallas_call(
        functools.partial(_post_kernel, seq_per_chunk=seq_per_chunk),
        grid=(t // tm,),
        in_specs=[_rows(tm, w) for w in widths] + [_resident(c.shape) for c in consts],
        out_specs=_rows(tm, D_MODEL),
        out_shape=jax.ShapeDtypeStruct((t, D_MODEL), F32),
        compiler_params=pltpu.CompilerParams(dimension_semantics=("parallel",),
                                             vmem_limit_bytes=V7X_VMEM_LIMIT_BYTES),
        name="post_stage",
    )(h, a, gu, vs, ga, gb, *consts)


def kernel(x_prompt, x_sample, cache_k, cache_v, page_table, norm_ffn1, ffn1_w_in, ffn1_w_out,
           norm_mix, w_in, lambda_q1, lambda_k1, lambda_q2, lambda_k2, subln, sgu_ln_g, sgu_ln_b,
           sgu_w, sgu_b, w_proj_attn, w_proj_sgu, w_out, norm_ffn2, ffn2_w_in, ffn2_w_out,
           norm_final):
    assert norm_ffn1.shape[0] == 1, "single-layer stack"
    b, s, d = x_prompt.shape
    db, nt, _ = x_sample.shape
    assert CHUNK % nt == 0 and (db * nt) % CHUNK == 0
    tail_start = ((s - 1) // CHUNK) * CHUNK

    bf = lambda w: w[0].astype(BF16)
    pre_w = (norm_ffn1, bf(ffn1_w_in), bf(ffn1_w_out), norm_mix, bf(w_in), sgu_ln_g, sgu_ln_b)
    lam_w = (lambda_q1, lambda_k1, lambda_q2, lambda_k2, subln)
    post_w = (bf(w_proj_attn), bf(w_proj_sgu), bf(w_out), norm_ffn2, bf(ffn2_w_in), bf(ffn2_w_out),
              norm_final.reshape(1, d))

    h, k, v, gu, vs, ga, gb, qh, kh, vh = _pre_stage(x_prompt.reshape(b * s, d), *pre_w,
                                                     head_major_seq=s)
    a = _attn_prompt(qh, kh, vh, *lam_w)
    y_prompt = _post_stage(h, a.reshape(b * s, ATT_WIDTH), gu, vs, ga, gb,
                           sgu_w[0], sgu_b[0].T, *post_w, seq_per_chunk=1)
    k_prompt = k.reshape(1, b, s, N_HEADS, KEY_DIM)
    v_prompt = v.reshape(1, b, s, N_HEADS, VALUE_DIM)
    sgu_v_prompt = vs.reshape(b, s, SGU_WIDTH)[None, :, tail_start:]

    reps = CHUNK // nt
    wsp_s = jnp.tile(sgu_w[0][:, :nt, :nt], (1, reps, reps))
    bsp_s = jnp.tile(sgu_b[0][:, :nt], (1, reps)).T
    hs, ks, vsm, gus, vss, gas, gbs, qs = _pre_stage(x_sample.reshape(db * nt, d), *pre_w)
    n_pool = cache_k.shape[1]
    a_s = _attn_sample(qs.reshape(db, nt, Q_COLS), ks, vsm,
                       cache_k.reshape(n_pool, PAGE_SIZE * N_HEADS, KEY_DIM),
                       cache_v.reshape(n_pool, PAGE_SIZE * N_HEADS, VALUE_DIM), page_table, *lam_w)
    y_sample = _post_stage(hs, a_s.reshape(db * nt, ATT_WIDTH), gus, vss, gas, gbs,
                           wsp_s, bsp_s, *post_w, seq_per_chunk=reps)
    k_sample = ks.reshape(1, db, nt, N_HEADS, KEY_DIM)
    v_sample = vsm.reshape(1, db, nt, N_HEADS, VALUE_DIM)
    sgu_v_sample = vss.reshape(1, db, nt, SGU_WIDTH)

    return (y_prompt.reshape(b, s, d), y_sample.reshape(db, nt, d), k_prompt, v_prompt,
            sgu_v_prompt, k_sample, v_sample, sgu_v_sample)
```

```python
import functools
import math

import jax
import jax.numpy as jnp
from jax import lax
from jax.experimental import pallas as pl
from jax.experimental.pallas import tpu as pltpu

D_MODEL = 1024
D_FF = 2816
N_HEADS = 4
HEAD_DIM = 64
KEY_DIM = 2 * HEAD_DIM
VALUE_DIM = 2 * HEAD_DIM
Q_COLS = N_HEADS * KEY_DIM
ATT_WIDTH = N_HEADS * VALUE_DIM
SGU_WIDTH = 512
SGU_GROUPS = 4
SGU_GROUP_DIM = SGU_WIDTH // SGU_GROUPS
CHUNK = 128
PAGE_SIZE = 128
SCALE = HEAD_DIM ** -0.5
RMS_EPS = 1e-6
LN_EPS = 1e-5
LAMBDA_INIT = 0.8 - 0.6 * math.exp(-0.3 * 0)
SQRT_HALF = math.sqrt(0.5)
LOG2_E = math.log2(math.e)

_Q0, _K0, _V0, _U0, _VG0, _GA0, _GB0, _END = 0, 512, 1024, 1536, 2048, 2560, 3584, 4608

BF16 = jnp.bfloat16
F32 = jnp.float32

V7X_VMEM_LIMIT_BYTES = 60 * 1024 * 1024

TOKEN_TILE = 256
ATTN_Q_TILE = 512
ATTN_KV_TILE = 256
ATTN_COL_GROUP = 256
ATTN_SUM_ROWS = 16
PAGES_PER_STEP = 16
SAMPLE_COLS = 128


def _rms(x, g, eps=RMS_EPS):
    return x * lax.rsqrt(jnp.mean(x * x, axis=-1, keepdims=True) + eps) * g


def _gelu(x):
    return 0.5 * x * (1.0 + lax.erf(x * SQRT_HALF))


def _swiglu_half_step(x, g_ref, w_in_ref, w_out_ref):
    xn = _rms(x, g_ref[...]).astype(BF16)
    gate_up = jnp.dot(xn, w_in_ref[...], preferred_element_type=F32)
    gate = gate_up[:, :D_FF]
    up = gate_up[:, D_FF:]
    act = (gate * jax.nn.sigmoid(gate) * up).astype(BF16)
    return x + 0.5 * jnp.dot(act, w_out_ref[...], preferred_element_type=F32)


def _diff_lambda(lq1_ref, lk1_ref, lq2_ref, lk2_ref):
    s1 = jnp.sum(lq1_ref[...] * lk1_ref[...], axis=-1, keepdims=True)
    s2 = jnp.sum(lq2_ref[...] * lk2_ref[...], axis=-1, keepdims=True)
    return jnp.exp(s1) - jnp.exp(s2) + LAMBDA_INIT


def _sub_norm(o1, o2, lam, subln):
    o = o1 - lam * o2
    return _rms(o, subln) * (1.0 - LAMBDA_INIT)


def _pre_kernel(x_ref, n1_ref, w1_ref, w2_ref, nm_ref, win_ref, lng_ref, lnb_ref,
                h_ref, k_ref, v_ref, gu_ref, vs_ref, ga_ref, gb_ref, *attn_refs):
    h = _swiglu_half_step(x_ref[...], n1_ref, w1_ref, w2_ref)
    h_ref[...] = h
    nrm = _rms(h, nm_ref[...]).astype(BF16)

    def proj(lo, hi):
        return jnp.dot(nrm, win_ref[:, lo:hi], preferred_element_type=F32)

    q = proj(_Q0, _K0) * (SCALE * LOG2_E)
    k = proj(_K0, _V0)
    v = proj(_V0, _U0)
    head = lambda z, hd: z[:, hd * KEY_DIM:(hd + 1) * KEY_DIM]
    tm = x_ref.shape[0]
    for hd in range(N_HEADS):
        k_ref[pl.ds(hd, tm, stride=N_HEADS), :] = head(k, hd)
        v_ref[pl.ds(hd, tm, stride=N_HEADS), :] = head(v, hd)
    if len(attn_refs) == 3:
        qt_ref, kh_ref, vt_ref = attn_refs
        for hd in range(N_HEADS):
            qt_ref[hd] = head(q, hd).T.astype(BF16)
            kh_ref[hd] = head(k, hd).astype(BF16)
            vt_ref[hd] = head(v, hd).T.astype(BF16)
    else:
        attn_refs[0][...] = q
    gu_ref[...] = _gelu(proj(_U0, _VG0))
    gv = _gelu(proj(_VG0, _GA0))
    mu = jnp.mean(gv, axis=-1, keepdims=True)
    gc = gv - mu
    vs_ref[...] = (gc * lax.rsqrt(jnp.mean(gc * gc, axis=-1, keepdims=True) + LN_EPS)
                   * lng_ref[...] + lnb_ref[...])
    ga_ref[...] = jax.nn.sigmoid(proj(_GA0, _GB0))
    gb_ref[...] = jax.nn.sigmoid(proj(_GB0, _END))


def _resident(shape):
    return pl.BlockSpec(shape, lambda *_: (0,) * len(shape), pipeline_mode=pl.Buffered(1))


def _rows(tm, width):
    return pl.BlockSpec((tm, width), lambda i: (i, 0))


def _pre_stage(x, n1, w1, w2, nm, win, lng, lnb, *, head_major_seq=None):
    t = x.shape[0]
    tm = TOKEN_TILE
    assert t % tm == 0
    cache_rows = _rows(tm * N_HEADS, KEY_DIM)
    cache_shape = jax.ShapeDtypeStruct((t * N_HEADS, KEY_DIM), F32)
    flat = lambda w: (_rows(tm, w), jax.ShapeDtypeStruct((t, w), F32))
    outs = [flat(D_MODEL), (cache_rows, cache_shape), (cache_rows, cache_shape), flat(SGU_WIDTH),
            flat(SGU_WIDTH), flat(D_MODEL), flat(D_MODEL)]
    if head_major_seq is not None:
        seq = head_major_seq
        assert seq % tm == 0 and t % seq == 0
        per = seq // tm
        rows = (pl.BlockSpec((None, N_HEADS, tm, KEY_DIM), lambda i: (i // per, 0, i % per, 0)),
                jax.ShapeDtypeStruct((t // seq, N_HEADS, seq, KEY_DIM), BF16))
        cols = (pl.BlockSpec((None, N_HEADS, KEY_DIM, tm), lambda i: (i // per, 0, 0, i % per)),
                jax.ShapeDtypeStruct((t // seq, N_HEADS, KEY_DIM, seq), BF16))
        outs += [cols, rows, cols]
    else:
        outs += [flat(Q_COLS)]
    return pl.pallas_call(
        _pre_kernel,
        grid=(t // tm,),
        in_specs=[_rows(tm, D_MODEL), _resident(n1.shape), _resident(w1.shape), _resident(w2.shape),
                  _resident(nm.shape), _resident(win.shape), _resident(lng.shape), _resident(lnb.shape)],
        out_specs=[o[0] for o in outs],
        out_shape=[o[1] for o in outs],
        compiler_params=pltpu.CompilerParams(dimension_semantics=("parallel",),
                                             vmem_limit_bytes=V7X_VMEM_LIMIT_BYTES),
        name="pre_stage",
    )(x, n1, w1, w2, nm, win, lng, lnb)


def _attn_prompt_kernel(qt_ref, k_ref, vt_ref, lq1_ref, lk1_ref, lq2_ref, lk2_ref, subln_t_ref,
                        o_ref, m_ref, acc_ref, alpha_ref, *bufs):
    qi = pl.program_id(2)
    tq, tk = ATTN_Q_TILE, ATTN_KV_TILE
    assert tq == 2 * tk
    st, pt = bufs[:4], bufs[4:]

    qt = qt_ref[...]
    dim = lax.broadcasted_iota(jnp.int32, qt.shape, 0)
    zero = jnp.zeros_like(qt)
    qqt = jnp.concatenate([jnp.where(dim < HEAD_DIM, qt, zero),
                           jnp.where(dim >= HEAD_DIM, qt, zero)], axis=1)

    groups = [slice(c, c + ATTN_COL_GROUP) for c in range(0, 2 * tq, ATTN_COL_GROUP)]

    def scores(blk, st_ref, cols):
        start = pl.multiple_of(blk * tk, tk)
        st_ref[:, cols] = jnp.dot(k_ref[pl.ds(start, tk), :], qqt[:, cols],
                                  preferred_element_type=F32)

    ones_rows = jnp.ones((ATTN_SUM_ROWS, tk), BF16)

    def weighted(blk, pt_ref, cols):
        start = pl.multiple_of(blk * tk, tk)
        lhs = jnp.concatenate([vt_ref[:, pl.ds(start, tk)], ones_rows], axis=0)
        return jnp.dot(lhs, pt_ref[:, cols], preferred_element_type=F32)

    def softmax(st_ref, pt_ref, cols, key_offset=None):
        s = st_ref[:, cols]
        if key_offset is not None:
            key = lax.broadcasted_iota(jnp.int32, s.shape, 0) + key_offset
            qpos = lax.broadcasted_iota(jnp.int32, s.shape, 1) + cols.start % tq
            s = jnp.where(key <= qpos, s, -jnp.inf)
        m_old = m_ref[:, cols]
        m_new = jnp.maximum(m_old, jnp.max(s, axis=0, keepdims=True))
        alpha = jnp.exp2(m_old - m_new)
        p = jnp.exp2(s - m_new)
        m_ref[:, cols] = m_new
        pt_ref[:, cols] = p.astype(BF16)
        return alpha

    def fold_pending(t, half, cols, alpha_a, alpha_b):
        pv_a = weighted(jnp.maximum(t - 2, 0), pt[2 * half], cols)
        pv_b = weighted(jnp.maximum(t - 1, 0), pt[2 * half + 1], cols)
        acc = acc_ref[:, cols] + alpha_ref[:, cols] * pv_a + pv_b
        return (alpha_a * alpha_b) * acc

    def pair(t, parity):
        cur, nxt = parity, 1 - parity
        for cols in groups:
            scores(t + 2, st[2 * nxt], cols)
            scores(t + 3, st[2 * nxt + 1], cols)
            alpha_a = softmax(st[2 * cur], pt[2 * cur], cols)
            alpha_b = softmax(st[2 * cur + 1], pt[2 * cur + 1], cols)
            acc_ref[:, cols] = fold_pending(t, nxt, cols, alpha_a, alpha_b)
            alpha_ref[:, cols] = alpha_b

    def diagonal(t, parity):
        cur, nxt = parity, 1 - parity
        for cols in groups:
            alpha_a = softmax(st[2 * cur], pt[2 * cur], cols, key_offset=0)
            alpha_b = softmax(st[2 * cur + 1], pt[2 * cur + 1], cols, key_offset=tk)
            acc = fold_pending(t, nxt, cols, alpha_a, alpha_b)
            acc_ref[:, cols] = (acc + alpha_b * weighted(t, pt[2 * cur], cols)
                                + weighted(t + 1, pt[2 * cur + 1], cols))

    m_ref[...] = jnp.full(m_ref.shape, -jnp.inf, F32)
    acc_ref[...] = jnp.zeros(acc_ref.shape, F32)
    alpha_ref[...] = jnp.ones(alpha_ref.shape, F32)
    pt[2][...] = jnp.zeros(pt[2].shape, BF16)
    pt[3][...] = jnp.zeros(pt[3].shape, BF16)
    for cols in groups:
        scores(0, st[0], cols)
        scores(1, st[1], cols)

    def body(u, carry):
        for parity in (0, 1):
            @pl.when(u % 2 == parity)
            def _():
                pair(2 * u, parity)
        return carry

    lax.fori_loop(0, qi, body, 0)
    for parity in (0, 1):
        @pl.when(qi % 2 == parity)
        def _():
            diagonal(2 * qi, parity)

    ot = acc_ref[:VALUE_DIM, :] / acc_ref[VALUE_DIM:VALUE_DIM + 1, :]
    lam = _diff_lambda(lq1_ref, lk1_ref, lq2_ref, lk2_ref)
    o = ot[:, :tq] - lam * ot[:, tq:]
    y = (o * lax.rsqrt(jnp.mean(o * o, axis=0, keepdims=True) + RMS_EPS) * subln_t_ref[...]
         * (1.0 - LAMBDA_INIT))
    o_ref[...] = y.T.astype(BF16)


def _attn_prompt(qt, k, vt, lq1, lk1, lq2, lk2, subln_t):
    b, _, s, _ = k.shape
    tq = ATTN_Q_TILE
    tk = ATTN_KV_TILE
    assert s % tq == 0
    small = lambda a: pl.BlockSpec(a.shape, lambda bi, hi, qi: (0, 0))
    return pl.pallas_call(
        _attn_prompt_kernel,
        grid=(b, N_HEADS, s // tq),
        in_specs=[pl.BlockSpec((None, None, KEY_DIM, tq), lambda bi, hi, qi: (bi, hi, 0, qi)),
                  pl.BlockSpec((None, None, s, KEY_DIM), lambda bi, hi, qi: (bi, hi, 0, 0)),
                  pl.BlockSpec((None, None, VALUE_DIM, s), lambda bi, hi, qi: (bi, hi, 0, 0)),
                  small(lq1), small(lk1), small(lq2), small(lk2), small(subln_t)],
        out_specs=pl.BlockSpec((None, tq, VALUE_DIM), lambda bi, hi, qi: (bi, qi, hi)),
        out_shape=jax.ShapeDtypeStruct((b, s, ATT_WIDTH), BF16),
        scratch_shapes=[pltpu.VMEM((1, 2 * tq), F32),
                        pltpu.VMEM((VALUE_DIM + ATTN_SUM_ROWS, 2 * tq), F32),
                        pltpu.VMEM((1, 2 * tq), F32)]
                       + [pltpu.VMEM((tk, 2 * tq), F32)] * 4 + [pltpu.VMEM((tk, 2 * tq), BF16)] * 4,
        compiler_params=pltpu.CompilerParams(
            dimension_semantics=("parallel", "parallel", "parallel"),
            vmem_limit_bytes=V7X_VMEM_LIMIT_BYTES),
        name="attn_prompt",
    )(qt, k, vt, lq1, lk1, lq2, lk2, subln_t)


def _attn_sample_kernel(pt_ref, q_ref, kn_ref, vn_ref, lq1_ref, lk1_ref, lq2_ref, lk2_ref,
                        subln_ref, *rest):
    del pt_ref
    npg = PAGES_PER_STEP
    k_pages, v_pages = rest[:npg], rest[npg:2 * npg]
    o_ref, m_ref, l_ref, acc_ref, kb_ref, vb_ref = rest[2 * npg:]
    c = pl.program_id(1)
    nt = q_ref.shape[0]
    ncol = SAMPLE_COLS

    q = q_ref[...]
    reps = ncol // nt
    qrows = jnp.concatenate([q] * reps, axis=0)
    rr = lax.broadcasted_iota(jnp.int32, qrows.shape, 1)
    cc = lax.broadcasted_iota(jnp.int32, qrows.shape, 0)
    live = (rr // HEAD_DIM == cc // nt) & (cc < 2 * nt * N_HEADS)
    qm_f32 = jnp.where(live, qrows, 0.0)
    qm = qm_f32.astype(BF16)
    contract_last = (((1,), (1,)), ((), ()))
    contract_first = (((0,), (0,)), ((), ()))

    @pl.when(c == 0)
    def _():
        head_rows = lambda ref, hd: ref[pl.ds(hd, nt, stride=N_HEADS), :]
        kn = jnp.concatenate([head_rows(kn_ref, hd) for hd in range(N_HEADS)], axis=1)
        vn = jnp.concatenate([head_rows(vn_ref, hd) for hd in range(N_HEADS)], axis=1)
        s = lax.dot_general(kn, qm_f32, contract_last, preferred_element_type=F32)
        key_t = lax.broadcasted_iota(jnp.int32, s.shape, 0)
        qry_t = lax.broadcasted_iota(jnp.int32, s.shape, 1) % nt
        s = jnp.where(key_t <= qry_t, s, -jnp.inf)
        m = jnp.max(s, axis=0, keepdims=True)
        p = jnp.exp2(s - m)
        m_ref[...] = m
        l_ref[...] = jnp.sum(p, axis=0, keepdims=True)
        acc_ref[...] = lax.dot_general(vn, p, contract_first, preferred_element_type=F32)

    for i in range(npg):
        rows = slice(i * PAGE_SIZE, (i + 1) * PAGE_SIZE)
        for hd in range(N_HEADS):
            lanes = slice(hd * KEY_DIM, (hd + 1) * KEY_DIM)
            tokens = pl.ds(hd, PAGE_SIZE, stride=N_HEADS)
            kb_ref[rows, lanes] = k_pages[i][tokens, :].astype(BF16)
            vb_ref[rows, lanes] = v_pages[i][tokens, :].astype(BF16)

    s = lax.dot_general(kb_ref[...], qm, contract_last, preferred_element_type=F32)
    m_old = m_ref[...]
    m_new = jnp.maximum(m_old, jnp.max(s, axis=0, keepdims=True))
    alpha = jnp.exp2(m_old - m_new)
    p = jnp.exp2(s - m_new)
    m_ref[...] = m_new
    l_ref[...] = alpha * l_ref[...] + jnp.sum(p, axis=0, keepdims=True)
    acc_ref[...] = alpha * acc_ref[...] + lax.dot_general(
        vb_ref[...], p.astype(BF16), contract_first, preferred_element_type=F32)

    @pl.when(c == pl.num_programs(1) - 1)
    def _():
        o = (acc_ref[...] / l_ref[...]).T
        lam = _diff_lambda(lq1_ref, lk1_ref, lq2_ref, lk2_ref)
        for h in range(N_HEADS):
            r0 = h * 2 * nt
            o1 = o[r0:r0 + nt, h * VALUE_DIM:(h + 1) * VALUE_DIM]
            o2 = o[r0 + nt:r0 + 2 * nt, h * VALUE_DIM:(h + 1) * VALUE_DIM]
            o_ref[:, h * VALUE_DIM:(h + 1) * VALUE_DIM] = _sub_norm(
                o1, o2, lam, subln_ref[...])


def _attn_sample(q, k_new, v_new, cache_k, cache_v, page_table, lq1, lk1, lq2, lk2, subln):
    db, nt, _ = q.shape
    n_pages = page_table.shape[1]
    npg = PAGES_PER_STEP
    assert n_pages % npg == 0 and SAMPLE_COLS % nt == 0 and 2 * nt * N_HEADS <= SAMPLE_COLS
    small = lambda a: pl.BlockSpec(a.shape, lambda b, c, pt: (0, 0))
    per_seq = lambda w: pl.BlockSpec((None, nt, w), lambda b, c, pt: (b, 0, 0))

    new_rows = pl.BlockSpec((nt * N_HEADS, KEY_DIM), lambda b, c, pt: (b, 0))

    def page(i):
        return pl.BlockSpec((None, PAGE_SIZE * N_HEADS, KEY_DIM),
                            lambda b, c, pt: (pt[b, c * npg + i], 0, 0))

    grid_spec = pltpu.PrefetchScalarGridSpec(
        num_scalar_prefetch=1,
        grid=(db, n_pages // npg),
        in_specs=([per_seq(Q_COLS), new_rows, new_rows,
                   small(lq1), small(lk1), small(lq2), small(lk2), small(subln)]
                  + [page(i) for i in range(npg)] * 2),
        out_specs=per_seq(ATT_WIDTH),
        scratch_shapes=[pltpu.VMEM((1, SAMPLE_COLS), F32), pltpu.VMEM((1, SAMPLE_COLS), F32),
                        pltpu.VMEM((ATT_WIDTH, SAMPLE_COLS), F32),
                        pltpu.VMEM((npg * PAGE_SIZE, Q_COLS), BF16),
                        pltpu.VMEM((npg * PAGE_SIZE, ATT_WIDTH), BF16)],
    )
    return pl.pallas_call(
        _attn_sample_kernel,
        grid_spec=grid_spec,
        out_shape=jax.ShapeDtypeStruct((db, nt, ATT_WIDTH), F32),
        compiler_params=pltpu.CompilerParams(dimension_semantics=("parallel", "arbitrary"),
                                             vmem_limit_bytes=V7X_VMEM_LIMIT_BYTES),
        name="attn_sample",
    )(page_table, q, k_new, v_new, lq1, lk1, lq2, lk2, subln,
      *([cache_k] * npg), *([cache_v] * npg))


def _post_kernel(h_ref, a_ref, gu_ref, vs_ref, ga_ref, gb_ref, wsp_ref, bsp_ref, wpa_ref, wpb_ref,
                 wo_ref, n2_ref, w1_ref, w2_ref, nf_ref, y_ref, *, seq_per_chunk):
    tm = h_ref.shape[0]
    row = lax.broadcasted_iota(jnp.int32, (CHUNK, CHUNK), 0)
    col = lax.broadcasted_iota(jnp.int32, (CHUNK, CHUNK), 1)
    seq_len = CHUNK // seq_per_chunk
    keep = (col <= row) & (row // seq_len == col // seq_len)
    bias = bsp_ref[...]
    mixed = []
    for ci in range(tm // CHUNK):
        groups = []
        for g in range(SGU_GROUPS):
            w = jnp.where(keep, wsp_ref[g], 0.0).astype(BF16)
            vg = vs_ref[ci * CHUNK:(ci + 1) * CHUNK,
                        g * SGU_GROUP_DIM:(g + 1) * SGU_GROUP_DIM].astype(BF16)
            groups.append(jnp.dot(w, vg, preferred_element_type=F32) + bias[:, g:g + 1])
        mixed.append(jnp.concatenate(groups, axis=1))
    sp = jnp.concatenate(mixed, axis=0) if len(mixed) > 1 else mixed[0]
    sgu = (gu_ref[...] * sp).astype(BF16)
    m = (ga_ref[...] * jnp.dot(a_ref[...].astype(BF16), wpa_ref[...], preferred_element_type=F32)
         + gb_ref[...] * jnp.dot(sgu, wpb_ref[...], preferred_element_type=F32))
    h = h_ref[...] + jnp.dot(m.astype(BF16), wo_ref[...], preferred_element_type=F32)
    out = _swiglu_half_step(h, n2_ref, w1_ref, w2_ref)
    y_ref[...] = _rms(out, nf_ref[...])


def _post_stage(h, a, gu, vs, ga, gb, wsp, bsp, wpa, wpb, wo, n2, w1, w2, nf, *, seq_per_chunk):
    t = h.shape[0]
    tm = TOKEN_TILE
    assert t % tm == 0 and tm % CHUNK == 0
    widths = (D_MODEL, ATT_WIDTH, SGU_WIDTH, SGU_WIDTH, D_MODEL, D_MODEL)
    consts = (wsp, bsp, wpa, wpb, wo, n2, w1, w2, nf)
    return pl.pallas_call(
        functools.partial(_post_kernel, seq_per_chunk=seq_per_chunk),
        grid=(t // tm,),
        in_specs=[_rows(tm, w) for w in widths] + [_resident(c.shape) for c in consts],
        out_specs=_rows(tm, D_MODEL),
        out_shape=jax.ShapeDtypeStruct((t, D_MODEL), F32),
        compiler_params=pltpu.CompilerParams(dimension_semantics=("parallel",),
                                             vmem_limit_bytes=V7X_VMEM_LIMIT_BYTES),
        name="post_stage",
    )(h, a, gu, vs, ga, gb, *consts)


def kernel(x_prompt, x_sample, cache_k, cache_v, page_table, norm_ffn1, ffn1_w_in, ffn1_w_out,
           norm_mix, w_in, lambda_q1, lambda_k1, lambda_q2, lambda_k2, subln, sgu_ln_g, sgu_ln_b,
           sgu_w, sgu_b, w_proj_attn, w_proj_sgu, w_out, norm_ffn2, ffn2_w_in, ffn2_w_out,
           norm_final):
    assert norm_ffn1.shape[0] == 1, "single-layer stack"
    b, s, d = x_prompt.shape
    db, nt, _ = x_sample.shape
    assert CHUNK % nt == 0 and (db * nt) % CHUNK == 0
    tail_start = ((s - 1) // CHUNK) * CHUNK

    bf = lambda w: w[0].astype(BF16)
    pre_w = (norm_ffn1, bf(ffn1_w_in), bf(ffn1_w_out), norm_mix, bf(w_in), sgu_ln_g, sgu_ln_b)
    lam_w = (lambda_q1, lambda_k1, lambda_q2, lambda_k2, subln)
    post_w = (bf(w_proj_attn), bf(w_proj_sgu), bf(w_out), norm_ffn2, bf(ffn2_w_in), bf(ffn2_w_out),
              norm_final.reshape(1, d))

    h, k, v, gu, vs, ga, gb, qh, kh, vh = _pre_stage(x_prompt.reshape(b * s, d), *pre_w,
                                                     head_major_seq=s)
    a = _attn_prompt(qh, kh, vh, *lam_w[:4], subln.reshape(VALUE_DIM, 1))
    y_prompt = _post_stage(h, a.reshape(b * s, ATT_WIDTH), gu, vs, ga, gb,
                           sgu_w[0], sgu_b[0].T, *post_w, seq_per_chunk=1)
    k_prompt = k.reshape(1, b, s, N_HEADS, KEY_DIM)
    v_prompt = v.reshape(1, b, s, N_HEADS, VALUE_DIM)
    sgu_v_prompt = vs.reshape(b, s, SGU_WIDTH)[None, :, tail_start:]

    reps = CHUNK // nt
    wsp_s = jnp.tile(sgu_w[0][:, :nt, :nt], (1, reps, reps))
    bsp_s = jnp.tile(sgu_b[0][:, :nt], (1, reps)).T
    hs, ks, vsm, gus, vss, gas, gbs, qs = _pre_stage(x_sample.reshape(db * nt, d), *pre_w)
    n_pool = cache_k.shape[1]
    a_s = _attn_sample(qs.reshape(db, nt, Q_COLS), ks, vsm,
                       cache_k.reshape(n_pool, PAGE_SIZE * N_HEADS, KEY_DIM),
                       cache_v.reshape(n_pool, PAGE_SIZE * N_HEADS, VALUE_DIM), page_table, *lam_w)
    y_sample = _post_stage(hs, a_s.reshape(db * nt, ATT_WIDTH), gus, vss, gas, gbs,
                           wsp_s, bsp_s, *post_w, seq_per_chunk=reps)
    k_sample = ks.reshape(1, db, nt, N_HEADS, KEY_DIM)
    v_sample = vsm.reshape(1, db, nt, N_HEADS, VALUE_DIM)
    sgu_v_sample = vss.reshape(1, db, nt, SGU_WIDTH)

    return (y_prompt.reshape(b, s, d), y_sample.reshape(db, nt, d), k_prompt, v_prompt,
            sgu_v_prompt, k_sample, v_sample, sgu_v_sample)
```

```python
import functools
import math

import jax
import jax.numpy as jnp
from jax import lax
from jax.experimental import pallas as pl
from jax.experimental.pallas import tpu as pltpu

D_MODEL = 1024
D_FF = 2816
N_HEADS = 4
HEAD_DIM = 64
KEY_DIM = 2 * HEAD_DIM
VALUE_DIM = 2 * HEAD_DIM
Q_COLS = N_HEADS * KEY_DIM
ATT_WIDTH = N_HEADS * VALUE_DIM
SGU_WIDTH = 512
SGU_GROUPS = 4
SGU_GROUP_DIM = SGU_WIDTH // SGU_GROUPS
CHUNK = 128
PAGE_SIZE = 128
SCALE = HEAD_DIM ** -0.5
RMS_EPS = 1e-6
LN_EPS = 1e-5
LAMBDA_INIT = 0.8 - 0.6 * math.exp(-0.3 * 0)
SQRT_HALF = math.sqrt(0.5)
LOG2_E = math.log2(math.e)

_Q0, _K0, _V0, _U0, _VG0, _GA0, _GB0, _END = 0, 512, 1024, 1536, 2048, 2560, 3584, 4608

BF16 = jnp.bfloat16
F32 = jnp.float32

V7X_VMEM_LIMIT_BYTES = 60 * 1024 * 1024

TOKEN_TILE = 256
ATTN_Q_TILE = 512
ATTN_KV_TILE = 256
ATTN_COL_GROUP = 256
ATTN_SUM_ROWS = 16
PAGES_PER_STEP = 16
SAMPLE_PARTS = 2
SAMPLE_ROWS = 128


def _rms(x, g, eps=RMS_EPS):
    return x * lax.rsqrt(jnp.mean(x * x, axis=-1, keepdims=True) + eps) * g


def _gelu(x):
    return 0.5 * x * (1.0 + lax.erf(x * SQRT_HALF))


def _swiglu_half_step(x, g_ref, w_in_ref, w_out_ref):
    xn = _rms(x, g_ref[...]).astype(BF16)
    gate_up = jnp.dot(xn, w_in_ref[...], preferred_element_type=F32)
    gate = gate_up[:, :D_FF]
    up = gate_up[:, D_FF:]
    act = (gate * jax.nn.sigmoid(gate) * up).astype(BF16)
    return x + 0.5 * jnp.dot(act, w_out_ref[...], preferred_element_type=F32)


def _diff_lambda(lq1_ref, lk1_ref, lq2_ref, lk2_ref):
    s1 = jnp.sum(lq1_ref[...] * lk1_ref[...], axis=-1, keepdims=True)
    s2 = jnp.sum(lq2_ref[...] * lk2_ref[...], axis=-1, keepdims=True)
    return jnp.exp(s1) - jnp.exp(s2) + LAMBDA_INIT


def _sub_norm(o1, o2, lam, subln):
    o = o1 - lam * o2
    return _rms(o, subln) * (1.0 - LAMBDA_INIT)


def _pre_kernel(x_ref, n1_ref, w1_ref, w2_ref, nm_ref, win_ref, lng_ref, lnb_ref,
                h_ref, k_ref, v_ref, gu_ref, vs_ref, ga_ref, gb_ref, *attn_refs):
    h = _swiglu_half_step(x_ref[...], n1_ref, w1_ref, w2_ref)
    h_ref[...] = h
    nrm = _rms(h, nm_ref[...]).astype(BF16)

    def proj(lo, hi):
        return jnp.dot(nrm, win_ref[:, lo:hi], preferred_element_type=F32)

    q = proj(_Q0, _K0) * (SCALE * LOG2_E)
    k = proj(_K0, _V0)
    v = proj(_V0, _U0)
    head = lambda z, hd: z[:, hd * KEY_DIM:(hd + 1) * KEY_DIM]
    tm = x_ref.shape[0]
    for hd in range(N_HEADS):
        k_ref[pl.ds(hd, tm, stride=N_HEADS), :] = head(k, hd)
        v_ref[pl.ds(hd, tm, stride=N_HEADS), :] = head(v, hd)
    if len(attn_refs) == 3:
        qt_ref, kh_ref, vt_ref = attn_refs
        for hd in range(N_HEADS):
            qt_ref[hd] = head(q, hd).T.astype(BF16)
            kh_ref[hd] = head(k, hd).astype(BF16)
            vt_ref[hd] = head(v, hd).T.astype(BF16)
    else:
        attn_refs[0][...] = q
    gu_ref[...] = _gelu(proj(_U0, _VG0))
    gv = _gelu(proj(_VG0, _GA0))
    mu = jnp.mean(gv, axis=-1, keepdims=True)
    gc = gv - mu
    vs_ref[...] = (gc * lax.rsqrt(jnp.mean(gc * gc, axis=-1, keepdims=True) + LN_EPS)
                   * lng_ref[...] + lnb_ref[...])
    ga_ref[...] = jax.nn.sigmoid(proj(_GA0, _GB0))
    gb_ref[...] = jax.nn.sigmoid(proj(_GB0, _END))


def _resident(shape):
    return pl.BlockSpec(shape, lambda *_: (0,) * len(shape), pipeline_mode=pl.Buffered(1))


def _rows(tm, width):
    return pl.BlockSpec((tm, width), lambda i: (i, 0))


def _pre_stage(x, n1, w1, w2, nm, win, lng, lnb, *, head_major_seq=None):
    t = x.shape[0]
    tm = TOKEN_TILE
    assert t % tm == 0
    cache_rows = _rows(tm * N_HEADS, KEY_DIM)
    cache_shape = jax.ShapeDtypeStruct((t * N_HEADS, KEY_DIM), F32)
    flat = lambda w: (_rows(tm, w), jax.ShapeDtypeStruct((t, w), F32))
    outs = [flat(D_MODEL), (cache_rows, cache_shape), (cache_rows, cache_shape), flat(SGU_WIDTH),
            flat(SGU_WIDTH), flat(D_MODEL), flat(D_MODEL)]
    if head_major_seq is not None:
        seq = head_major_seq
        assert seq % tm == 0 and t % seq == 0
        per = seq // tm
        rows = (pl.BlockSpec((None, N_HEADS, tm, KEY_DIM), lambda i: (i // per, 0, i % per, 0)),
                jax.ShapeDtypeStruct((t // seq, N_HEADS, seq, KEY_DIM), BF16))
        cols = (pl.BlockSpec((None, N_HEADS, KEY_DIM, tm), lambda i: (i // per, 0, 0, i % per)),
                jax.ShapeDtypeStruct((t // seq, N_HEADS, KEY_DIM, seq), BF16))
        outs += [cols, rows, cols]
    else:
        outs += [flat(Q_COLS)]
    return pl.pallas_call(
        _pre_kernel,
        grid=(t // tm,),
        in_specs=[_rows(tm, D_MODEL), _resident(n1.shape), _resident(w1.shape), _resident(w2.shape),
                  _resident(nm.shape), _resident(win.shape), _resident(lng.shape), _resident(lnb.shape)],
        out_specs=[o[0] for o in outs],
        out_shape=[o[1] for o in outs],
        compiler_params=pltpu.CompilerParams(dimension_semantics=("parallel",),
                                             vmem_limit_bytes=V7X_VMEM_LIMIT_BYTES),
        name="pre_stage",
    )(x, n1, w1, w2, nm, win, lng, lnb)


def _attn_prompt_kernel(qt_ref, k_ref, vt_ref, lq1_ref, lk1_ref, lq2_ref, lk2_ref, subln_t_ref,
                        o_ref, m_ref, acc_ref, alpha_ref, *bufs):
    qi = pl.program_id(2)
    tq, tk = ATTN_Q_TILE, ATTN_KV_TILE
    assert tq == 2 * tk
    st, pt = bufs[:4], bufs[4:]

    qt = qt_ref[...]
    dim = lax.broadcasted_iota(jnp.int32, qt.shape, 0)
    zero = jnp.zeros_like(qt)
    qqt = jnp.concatenate([jnp.where(dim < HEAD_DIM, qt, zero),
                           jnp.where(dim >= HEAD_DIM, qt, zero)], axis=1)

    groups = [slice(c, c + ATTN_COL_GROUP) for c in range(0, 2 * tq, ATTN_COL_GROUP)]

    def scores(blk, st_ref, cols):
        start = pl.multiple_of(blk * tk, tk)
        st_ref[:, cols] = jnp.dot(k_ref[pl.ds(start, tk), :], qqt[:, cols],
                                  preferred_element_type=F32)

    ones_rows = jnp.ones((ATTN_SUM_ROWS, tk), BF16)

    def weighted(blk, pt_ref, cols):
        start = pl.multiple_of(blk * tk, tk)
        lhs = jnp.concatenate([vt_ref[:, pl.ds(start, tk)], ones_rows], axis=0)
        return jnp.dot(lhs, pt_ref[:, cols], preferred_element_type=F32)

    def softmax(st_ref, pt_ref, cols, key_offset=None):
        s = st_ref[:, cols]
        if key_offset is not None:
            key = lax.broadcasted_iota(jnp.int32, s.shape, 0) + key_offset
            qpos = lax.broadcasted_iota(jnp.int32, s.shape, 1) + cols.start % tq
            s = jnp.where(key <= qpos, s, -jnp.inf)
        m_old = m_ref[:, cols]
        m_new = jnp.maximum(m_old, jnp.max(s, axis=0, keepdims=True))
        alpha = jnp.exp2(m_old - m_new)
        p = jnp.exp2(s - m_new)
        m_ref[:, cols] = m_new
        pt_ref[:, cols] = p.astype(BF16)
        return alpha

    def fold_pending(t, half, cols, alpha_a, alpha_b):
        pv_a = weighted(jnp.maximum(t - 2, 0), pt[2 * half], cols)
        pv_b = weighted(jnp.maximum(t - 1, 0), pt[2 * half + 1], cols)
        acc = acc_ref[:, cols] + alpha_ref[:, cols] * pv_a + pv_b
        return (alpha_a * alpha_b) * acc

    def pair(t, parity):
        cur, nxt = parity, 1 - parity
        for cols in groups:
            scores(t + 2, st[2 * nxt], cols)
            scores(t + 3, st[2 * nxt + 1], cols)
            alpha_a = softmax(st[2 * cur], pt[2 * cur], cols)
            alpha_b = softmax(st[2 * cur + 1], pt[2 * cur + 1], cols)
            acc_ref[:, cols] = fold_pending(t, nxt, cols, alpha_a, alpha_b)
            alpha_ref[:, cols] = alpha_b

    def diagonal(t, parity):
        cur, nxt = parity, 1 - parity
        for cols in groups:
            alpha_a = softmax(st[2 * cur], pt[2 * cur], cols, key_offset=0)
            alpha_b = softmax(st[2 * cur + 1], pt[2 * cur + 1], cols, key_offset=tk)
            acc = fold_pending(t, nxt, cols, alpha_a, alpha_b)
            acc_ref[:, cols] = (acc + alpha_b * weighted(t, pt[2 * cur], cols)
                                + weighted(t + 1, pt[2 * cur + 1], cols))

    m_ref[...] = jnp.full(m_ref.shape, -jnp.inf, F32)
    acc_ref[...] = jnp.zeros(acc_ref.shape, F32)
    alpha_ref[...] = jnp.ones(alpha_ref.shape, F32)
    pt[2][...] = jnp.zeros(pt[2].shape, BF16)
    pt[3][...] = jnp.zeros(pt[3].shape, BF16)
    for cols in groups:
        scores(0, st[0], cols)
        scores(1, st[1], cols)

    def body(u, carry):
        for parity in (0, 1):
            @pl.when(u % 2 == parity)
            def _():
                pair(2 * u, parity)
        return carry

    lax.fori_loop(0, qi, body, 0)
    for parity in (0, 1):
        @pl.when(qi % 2 == parity)
        def _():
            diagonal(2 * qi, parity)

    ot = acc_ref[:VALUE_DIM, :] / acc_ref[VALUE_DIM:VALUE_DIM + 1, :]
    lam = _diff_lambda(lq1_ref, lk1_ref, lq2_ref, lk2_ref)
    o = ot[:, :tq] - lam * ot[:, tq:]
    y = (o * lax.rsqrt(jnp.mean(o * o, axis=0, keepdims=True) + RMS_EPS) * subln_t_ref[...]
         * (1.0 - LAMBDA_INIT))
    o_ref[...] = y.T.astype(BF16)


def _attn_prompt(qt, k, vt, lq1, lk1, lq2, lk2, subln_t):
    b, _, s, _ = k.shape
    tq = ATTN_Q_TILE
    tk = ATTN_KV_TILE
    assert s % tq == 0
    small = lambda a: pl.BlockSpec(a.shape, lambda bi, hi, qi: (0, 0))
    return pl.pallas_call(
        _attn_prompt_kernel,
        grid=(b, N_HEADS, s // tq),
        in_specs=[pl.BlockSpec((None, None, KEY_DIM, tq), lambda bi, hi, qi: (bi, hi, 0, qi)),
                  pl.BlockSpec((None, None, s, KEY_DIM), lambda bi, hi, qi: (bi, hi, 0, 0)),
                  pl.BlockSpec((None, None, VALUE_DIM, s), lambda bi, hi, qi: (bi, hi, 0, 0)),
                  small(lq1), small(lk1), small(lq2), small(lk2), small(subln_t)],
        out_specs=pl.BlockSpec((None, tq, VALUE_DIM), lambda bi, hi, qi: (bi, qi, hi)),
        out_shape=jax.ShapeDtypeStruct((b, s, ATT_WIDTH), BF16),
        scratch_shapes=[pltpu.VMEM((1, 2 * tq), F32),
                        pltpu.VMEM((VALUE_DIM + ATTN_SUM_ROWS, 2 * tq), F32),
                        pltpu.VMEM((1, 2 * tq), F32)]
                       + [pltpu.VMEM((tk, 2 * tq), F32)] * 4 + [pltpu.VMEM((tk, 2 * tq), BF16)] * 4,
        compiler_params=pltpu.CompilerParams(
            dimension_semantics=("parallel", "parallel", "parallel"),
            vmem_limit_bytes=V7X_VMEM_LIMIT_BYTES),
        name="attn_prompt",
    )(qt, k, vt, lq1, lk1, lq2, lk2, subln_t)


def _attn_sample_kernel(pt_ref, q_ref, kn_ref, vn_ref, lq1_ref, lk1_ref, lq2_ref, lk2_ref,
                        subln_ref, *rest):
    del pt_ref
    npg = PAGES_PER_STEP
    k_pages, v_pages = rest[:npg], rest[npg:2 * npg]
    o_ref, m_ref, l_ref, acc_ref, kb_ref, vb_ref = rest[2 * npg:]
    c = pl.program_id(1)
    nt = q_ref.shape[0]
    nrow = SAMPLE_ROWS

    q = q_ref[...]
    qrows = jnp.concatenate([q] * (nrow // nt), axis=0)
    dim = lax.broadcasted_iota(jnp.int32, qrows.shape, 1)
    row = lax.broadcasted_iota(jnp.int32, qrows.shape, 0)
    live = (dim // HEAD_DIM == row // nt) & (row < 2 * nt * N_HEADS)
    qm_f32 = jnp.where(live, qrows, 0.0)
    qm = qm_f32.astype(BF16)
    contract_last = (((1,), (1,)), ((), ()))

    @pl.when(c == 0)
    def _():
        head_rows = lambda ref, hd: ref[pl.ds(hd, nt, stride=N_HEADS), :]
        kn = jnp.concatenate([head_rows(kn_ref, hd) for hd in range(N_HEADS)], axis=1)
        vn = jnp.concatenate([head_rows(vn_ref, hd) for hd in range(N_HEADS)], axis=1)
        s = lax.dot_general(qm_f32, kn, contract_last, preferred_element_type=F32)
        qry_t = lax.broadcasted_iota(jnp.int32, s.shape, 0) % nt
        key_t = lax.broadcasted_iota(jnp.int32, s.shape, 1)
        s = jnp.where(key_t <= qry_t, s, -jnp.inf)
        m = jnp.max(s, axis=1, keepdims=True)
        p = jnp.exp2(s - m)
        m_ref[...] = m
        l_ref[...] = jnp.sum(p, axis=1, keepdims=True)
        acc_ref[...] = jnp.dot(p, vn, preferred_element_type=F32)

    def relayout(pages, buf_ref, part):
        for i in range(part * pages_per_part, (part + 1) * pages_per_part):
            rows = slice(i * PAGE_SIZE, (i + 1) * PAGE_SIZE)
            for hd in range(N_HEADS):
                lanes = slice(hd * KEY_DIM, (hd + 1) * KEY_DIM)
                tokens = pl.ds(hd, PAGE_SIZE, stride=N_HEADS)
                buf_ref[rows, lanes] = pages[i][tokens, :].astype(BF16)

    pages_per_part = npg // SAMPLE_PARTS
    part_rows = [slice(i * pages_per_part * PAGE_SIZE, (i + 1) * pages_per_part * PAGE_SIZE)
                 for i in range(SAMPLE_PARTS)]
    scores = []
    for part, rows in enumerate(part_rows):
        relayout(k_pages, kb_ref, part)
        scores.append(lax.dot_general(qm, kb_ref[rows, :], contract_last,
                                      preferred_element_type=F32))
    m_old = m_ref[...]
    m_new = m_old
    for s in scores:
        m_new = jnp.maximum(m_new, jnp.max(s, axis=1, keepdims=True))
    alpha = jnp.exp2(m_old - m_new)
    m_ref[...] = m_new
    l = alpha * l_ref[...]
    acc = alpha * acc_ref[...]
    for part, rows in enumerate(part_rows):
        relayout(v_pages, vb_ref, part)
        p = jnp.exp2(scores[part] - m_new)
        l = l + jnp.sum(p, axis=1, keepdims=True)
        acc = acc + jnp.dot(p.astype(BF16), vb_ref[rows, :], preferred_element_type=F32)
    l_ref[...] = l
    acc_ref[...] = acc

    @pl.when(c == pl.num_programs(1) - 1)
    def _():
        o = acc_ref[...] / l_ref[...]
        lam = _diff_lambda(lq1_ref, lk1_ref, lq2_ref, lk2_ref)
        for h in range(N_HEADS):
            r0 = h * 2 * nt
            o1 = o[r0:r0 + nt, h * VALUE_DIM:(h + 1) * VALUE_DIM]
            o2 = o[r0 + nt:r0 + 2 * nt, h * VALUE_DIM:(h + 1) * VALUE_DIM]
            o_ref[:, h * VALUE_DIM:(h + 1) * VALUE_DIM] = _sub_norm(
                o1, o2, lam, subln_ref[...])


def _attn_sample(q, k_new, v_new, cache_k, cache_v, page_table, lq1, lk1, lq2, lk2, subln):
    db, nt, _ = q.shape
    n_pages = page_table.shape[1]
    npg = PAGES_PER_STEP
    assert n_pages % npg == 0 and npg % SAMPLE_PARTS == 0
    assert SAMPLE_ROWS % nt == 0 and 2 * nt * N_HEADS <= SAMPLE_ROWS
    small = lambda a: pl.BlockSpec(a.shape, lambda b, c, pt: (0, 0))
    per_seq = lambda w: pl.BlockSpec((None, nt, w), lambda b, c, pt: (b, 0, 0))

    new_rows = pl.BlockSpec((nt * N_HEADS, KEY_DIM), lambda b, c, pt: (b, 0))

    def page(i):
        return pl.BlockSpec((None, PAGE_SIZE * N_HEADS, KEY_DIM),
                            lambda b, c, pt: (pt[b, c * npg + i], 0, 0))

    grid_spec = pltpu.PrefetchScalarGridSpec(
        num_scalar_prefetch=1,
        grid=(db, n_pages // npg),
        in_specs=([per_seq(Q_COLS), new_rows, new_rows,
                   small(lq1), small(lk1), small(lq2), small(lk2), small(subln)]
                  + [page(i) for i in range(npg)] * 2),
        out_specs=per_seq(ATT_WIDTH),
        scratch_shapes=[pltpu.VMEM((SAMPLE_ROWS, 1), F32), pltpu.VMEM((SAMPLE_ROWS, 1), F32),
                        pltpu.VMEM((SAMPLE_ROWS, ATT_WIDTH), F32),
                        pltpu.VMEM((npg * PAGE_SIZE, Q_COLS), BF16),
                        pltpu.VMEM((npg * PAGE_SIZE, ATT_WIDTH), BF16)],
    )
    return pl.pallas_call(
        _attn_sample_kernel,
        grid_spec=grid_spec,
        out_shape=jax.ShapeDtypeStruct((db, nt, ATT_WIDTH), F32),
        compiler_params=pltpu.CompilerParams(dimension_semantics=("parallel", "arbitrary"),
                                             vmem_limit_bytes=V7X_VMEM_LIMIT_BYTES),
        name="attn_sample",
    )(page_table, q, k_new, v_new, lq1, lk1, lq2, lk2, subln,
      *([cache_k] * npg), *([cache_v] * npg))


def _post_kernel(h_ref, a_ref, gu_ref, vs_ref, ga_ref, gb_ref, wsp_ref, bsp_ref, wpa_ref, wpb_ref,
                 wo_ref, n2_ref, w1_ref, w2_ref, nf_ref, y_ref, *, seq_per_chunk):
    tm = h_ref.shape[0]
    row = lax.broadcasted_iota(jnp.int32, (CHUNK, CHUNK), 0)
    col = lax.broadcasted_iota(jnp.int32, (CHUNK, CHUNK), 1)
    seq_len = CHUNK // seq_per_chunk
    keep = (col <= row) & (row // seq_len == col // seq_len)
    bias = bsp_ref[...]
    mixed = []
    for ci in range(tm // CHUNK):
        groups = []
        for g in range(SGU_GROUPS):
            w = jnp.where(keep, wsp_ref[g], 0.0).astype(BF16)
            vg = vs_ref[ci * CHUNK:(ci + 1) * CHUNK,
                        g * SGU_GROUP_DIM:(g + 1) * SGU_GROUP_DIM].astype(BF16)
            groups.append(jnp.dot(w, vg, preferred_element_type=F32) + bias[:, g:g + 1])
        mixed.append(jnp.concatenate(groups, axis=1))
    sp = jnp.concatenate(mixed, axis=0) if len(mixed) > 1 else mixed[0]
    sgu = (gu_ref[...] * sp).astype(BF16)
    m = (ga_ref[...] * jnp.dot(a_ref[...].astype(BF16), wpa_ref[...], preferred_element_type=F32)
         + gb_ref[...] * jnp.dot(sgu, wpb_ref[...], preferred_element_type=F32))
    h = h_ref[...] + jnp.dot(m.astype(BF16), wo_ref[...], preferred_element_type=F32)
    out = _swiglu_half_step(h, n2_ref, w1_ref, w2_ref)
    y_ref[...] = _rms(out, nf_ref[...])


def _post_stage(h, a, gu, vs, ga, gb, wsp, bsp, wpa, wpb, wo, n2, w1, w2, nf, *, seq_per_chunk):
    t = h.shape[0]
    tm = TOKEN_TILE
    assert t % tm == 0 and tm % CHUNK == 0
    widths = (D_MODEL, ATT_WIDTH, SGU_WIDTH, SGU_WIDTH, D_MODEL, D_MODEL)
    consts = (wsp, bsp, wpa, wpb, wo, n2, w1, w2, nf)
    return pl.pallas_call(
        functools.partial(_post_kernel, seq_per_chunk=seq_per_chunk),
        grid=(t // tm,),
        in_specs=[_rows(tm, w) for w in widths] + [_resident(c.shape) for c in consts],
        out_specs=_rows(tm, D_MODEL),
        out_shape=jax.ShapeDtypeStruct((t, D_MODEL), F32),
        compiler_params=pltpu.CompilerParams(dimension_semantics=("parallel",),
                                             vmem_limit_bytes=V7X_VMEM_LIMIT_BYTES),
        name="post_stage",
    )(h, a, gu, vs, ga, gb, *consts)


def kernel(x_prompt, x_sample, cache_k, cache_v, page_table, norm_ffn1, ffn1_w_in, ffn1_w_out,
           norm_mix, w_in, lambda_q1, lambda_k1, lambda_q2, lambda_k2, subln, sgu_ln_g, sgu_ln_b,
           sgu_w, sgu_b, w_proj_attn, w_proj_sgu, w_out, norm_ffn2, ffn2_w_in, ffn2_w_out,
           norm_final):
    assert norm_ffn1.shape[0] == 1, "single-layer stack"
    b, s, d = x_prompt.shape
    db, nt, _ = x_sample.shape
    assert CHUNK % nt == 0 and (db * nt) % CHUNK == 0
    tail_start = ((s - 1) // CHUNK) * CHUNK

    bf = lambda w: w[0].astype(BF16)
    pre_w = (norm_ffn1, bf(ffn1_w_in), bf(ffn1_w_out), norm_mix, bf(w_in), sgu_ln_g, sgu_ln_b)
    lam_w = (lambda_q1, lambda_k1, lambda_q2, lambda_k2, subln)
    post_w = (bf(w_proj_attn), bf(w_proj_sgu), bf(w_out), norm_ffn2, bf(ffn2_w_in), bf(ffn2_w_out),
              norm_final.reshape(1, d))

    h, k, v, gu, vs, ga, gb, qh, kh, vh = _pre_stage(x_prompt.reshape(b * s, d), *pre_w,
                                                     head_major_seq=s)
    a = _attn_prompt(qh, kh, vh, *lam_w[:4], subln.reshape(VALUE_DIM, 1))
    y_prompt = _post_stage(h, a.reshape(b * s, ATT_WIDTH), gu, vs, ga, gb,
                           sgu_w[0], sgu_b[0].T, *post_w, seq_per_chunk=1)
    k_prompt = k.reshape(1, b, s, N_HEADS, KEY_DIM)
    v_prompt = v.reshape(1, b, s, N_HEADS, VALUE_DIM)
    sgu_v_prompt = vs.reshape(b, s, SGU_WIDTH)[None, :, tail_start:]

    reps = CHUNK // nt
    wsp_s = jnp.tile(sgu_w[0][:, :nt, :nt], (1, reps, reps))
    bsp_s = jnp.tile(sgu_b[0][:, :nt], (1, reps)).T
    hs, ks, vsm, gus, vss, gas, gbs, qs = _pre_stage(x_sample.reshape(db * nt, d), *pre_w)
    n_pool = cache_k.shape[1]
    a_s = _attn_sample(qs.reshape(db, nt, Q_COLS), ks, vsm,
                       cache_k.reshape(n_pool, PAGE_SIZE * N_HEADS, KEY_DIM),
                       cache_v.reshape(n_pool, PAGE_SIZE * N_HEADS, VALUE_DIM), page_table, *lam_w)
    y_sample = _post_stage(hs, a_s.reshape(db * nt, ATT_WIDTH), gus, vss, gas, gbs,
                           wsp_s, bsp_s, *post_w, seq_per_chunk=reps)
    k_sample = ks.reshape(1, db, nt, N_HEADS, KEY_DIM)
    v_sample = vsm.reshape(1, db, nt, N_HEADS, VALUE_DIM)
    sgu_v_sample = vss.reshape(1, db, nt, SGU_WIDTH)

    return (y_prompt.reshape(b, s, d), y_sample.reshape(db, nt, d), k_prompt, v_prompt,
            sgu_v_prompt, k_sample, v_sample, sgu_v_sample)
```

```python
import functools
import math

import jax
import jax.numpy as jnp
from jax import lax
from jax.experimental import pallas as pl
from jax.experimental.pallas import tpu as pltpu

D_MODEL = 1024
D_FF = 2816
N_HEADS = 4
HEAD_DIM = 64
KEY_DIM = 2 * HEAD_DIM
VALUE_DIM = 2 * HEAD_DIM
Q_COLS = N_HEADS * KEY_DIM
ATT_WIDTH = N_HEADS * VALUE_DIM
SGU_WIDTH = 512
SGU_GROUPS = 4
SGU_GROUP_DIM = SGU_WIDTH // SGU_GROUPS
CHUNK = 128
PAGE_SIZE = 128
SCALE = HEAD_DIM ** -0.5
RMS_EPS = 1e-6
LN_EPS = 1e-5
LAMBDA_INIT = 0.8 - 0.6 * math.exp(-0.3 * 0)
SQRT_HALF = math.sqrt(0.5)
LOG2_E = math.log2(math.e)

_Q0, _K0, _V0, _U0, _VG0, _GA0, _GB0, _END = 0, 512, 1024, 1536, 2048, 2560, 3584, 4608

BF16 = jnp.bfloat16
F32 = jnp.float32

V7X_VMEM_LIMIT_BYTES = 60 * 1024 * 1024

TOKEN_TILE = 256
ATTN_Q_TILE = 512
ATTN_KV_TILE = 256
ATTN_COL_GROUP = 256
ATTN_SUM_ROWS = 16
PAGES_PER_STEP = 16
SAMPLE_PARTS = 2
SAMPLE_ROWS = 128


def _rms(x, g, eps=RMS_EPS):
    return x * lax.rsqrt(jnp.mean(x * x, axis=-1, keepdims=True) + eps) * g


def _gelu(x):
    return 0.5 * x * (1.0 + lax.erf(x * SQRT_HALF))


def _swiglu_half_step(x, g_ref, w_in_ref, w_out_ref):
    xn = _rms(x, g_ref[...]).astype(BF16)
    gate_up = jnp.dot(xn, w_in_ref[...], preferred_element_type=F32)
    gate = gate_up[:, :D_FF]
    up = gate_up[:, D_FF:]
    act = (gate * jax.nn.sigmoid(gate) * up).astype(BF16)
    return x + 0.5 * jnp.dot(act, w_out_ref[...], preferred_element_type=F32)


def _diff_lambda(lq1_ref, lk1_ref, lq2_ref, lk2_ref):
    s1 = jnp.sum(lq1_ref[...] * lk1_ref[...], axis=-1, keepdims=True)
    s2 = jnp.sum(lq2_ref[...] * lk2_ref[...], axis=-1, keepdims=True)
    return jnp.exp(s1) - jnp.exp(s2) + LAMBDA_INIT


def _sub_norm(o1, o2, lam, subln):
    o = o1 - lam * o2
    return _rms(o, subln) * (1.0 - LAMBDA_INIT)


def _pre_kernel(x_ref, n1_ref, w1_ref, w2_ref, nm_ref, win_ref, lng_ref, lnb_ref,
                h_ref, k_ref, v_ref, gu_ref, vs_ref, ga_ref, gb_ref, *attn_refs):
    h = _swiglu_half_step(x_ref[...], n1_ref, w1_ref, w2_ref)
    h_ref[...] = h
    nrm = _rms(h, nm_ref[...]).astype(BF16)

    def proj(lo, hi):
        return jnp.dot(nrm, win_ref[:, lo:hi], preferred_element_type=F32)

    q = proj(_Q0, _K0) * (SCALE * LOG2_E)
    k = proj(_K0, _V0)
    v = proj(_V0, _U0)
    head = lambda z, hd: z[:, hd * KEY_DIM:(hd + 1) * KEY_DIM]
    tm = x_ref.shape[0]
    for hd in range(N_HEADS):
        k_ref[pl.ds(hd, tm, stride=N_HEADS), :] = head(k, hd)
        v_ref[pl.ds(hd, tm, stride=N_HEADS), :] = head(v, hd)
    if len(attn_refs) == 3:
        qt_ref, kh_ref, vt_ref = attn_refs
        for hd in range(N_HEADS):
            qt_ref[hd] = head(q, hd).T.astype(BF16)
            kh_ref[hd] = head(k, hd).astype(BF16)
            vt_ref[hd] = head(v, hd).T.astype(BF16)
    else:
        attn_refs[0][...] = q
    gu_ref[...] = _gelu(proj(_U0, _VG0))
    gv = _gelu(proj(_VG0, _GA0))
    mu = jnp.mean(gv, axis=-1, keepdims=True)
    gc = gv - mu
    vs_ref[...] = (gc * lax.rsqrt(jnp.mean(gc * gc, axis=-1, keepdims=True) + LN_EPS)
                   * lng_ref[...] + lnb_ref[...])
    ga_ref[...] = jax.nn.sigmoid(proj(_GA0, _GB0))
    gb_ref[...] = jax.nn.sigmoid(proj(_GB0, _END))


def _resident(shape):
    return pl.BlockSpec(shape, lambda *_: (0,) * len(shape), pipeline_mode=pl.Buffered(1))


def _rows(tm, width):
    return pl.BlockSpec((tm, width), lambda i: (i, 0))


def _pre_stage(x, n1, w1, w2, nm, win, lng, lnb, *, head_major_seq=None):
    t = x.shape[0]
    tm = TOKEN_TILE
    assert t % tm == 0
    cache_rows = _rows(tm * N_HEADS, KEY_DIM)
    cache_shape = jax.ShapeDtypeStruct((t * N_HEADS, KEY_DIM), F32)
    flat = lambda w: (_rows(tm, w), jax.ShapeDtypeStruct((t, w), F32))
    outs = [flat(D_MODEL), (cache_rows, cache_shape), (cache_rows, cache_shape), flat(SGU_WIDTH),
            flat(SGU_WIDTH), flat(D_MODEL), flat(D_MODEL)]
    if head_major_seq is not None:
        seq = head_major_seq
        assert seq % tm == 0 and t % seq == 0
        per = seq // tm
        rows = (pl.BlockSpec((None, N_HEADS, tm, KEY_DIM), lambda i: (i // per, 0, i % per, 0)),
                jax.ShapeDtypeStruct((t // seq, N_HEADS, seq, KEY_DIM), BF16))
        cols = (pl.BlockSpec((None, N_HEADS, KEY_DIM, tm), lambda i: (i // per, 0, 0, i % per)),
                jax.ShapeDtypeStruct((t // seq, N_HEADS, KEY_DIM, seq), BF16))
        outs += [cols, rows, cols]
    else:
        outs += [flat(Q_COLS)]
    return pl.pallas_call(
        _pre_kernel,
        grid=(t // tm,),
        in_specs=[_rows(tm, D_MODEL), _resident(n1.shape), _resident(w1.shape), _resident(w2.shape),
                  _resident(nm.shape), _resident(win.shape), _resident(lng.shape), _resident(lnb.shape)],
        out_specs=[o[0] for o in outs],
        out_shape=[o[1] for o in outs],
        compiler_params=pltpu.CompilerParams(dimension_semantics=("parallel",),
                                             vmem_limit_bytes=V7X_VMEM_LIMIT_BYTES),
        name="pre_stage",
    )(x, n1, w1, w2, nm, win, lng, lnb)


def _attn_prompt_body(qi, qt_ref, k_ref, vt_ref, lq1_ref, lk1_ref, lq2_ref, lk2_ref, subln_t_ref,
                      o_ref, m_ref, acc_ref, alpha_ref, *bufs):
    tq, tk = ATTN_Q_TILE, ATTN_KV_TILE
    assert tq == 2 * tk
    st, pt = bufs[:4], bufs[4:]

    qt = qt_ref[...]
    dim = lax.broadcasted_iota(jnp.int32, qt.shape, 0)
    zero = jnp.zeros_like(qt)
    qqt = jnp.concatenate([jnp.where(dim < HEAD_DIM, qt, zero),
                           jnp.where(dim >= HEAD_DIM, qt, zero)], axis=1)

    groups = [slice(c, c + ATTN_COL_GROUP) for c in range(0, 2 * tq, ATTN_COL_GROUP)]

    def scores(blk, st_ref, cols):
        start = pl.multiple_of(blk * tk, tk)
        st_ref[:, cols] = jnp.dot(k_ref[pl.ds(start, tk), :], qqt[:, cols],
                                  preferred_element_type=F32)

    ones_rows = jnp.ones((ATTN_SUM_ROWS, tk), BF16)

    def weighted(blk, pt_ref, cols):
        start = pl.multiple_of(blk * tk, tk)
        lhs = jnp.concatenate([vt_ref[:, pl.ds(start, tk)], ones_rows], axis=0)
        return jnp.dot(lhs, pt_ref[:, cols], preferred_element_type=F32)

    def softmax(st_ref, pt_ref, cols, key_offset=None):
        s = st_ref[:, cols]
        if key_offset is not None:
            key = lax.broadcasted_iota(jnp.int32, s.shape, 0) + key_offset
            qpos = lax.broadcasted_iota(jnp.int32, s.shape, 1) + cols.start % tq
            s = jnp.where(key <= qpos, s, -jnp.inf)
        m_old = m_ref[:, cols]
        m_new = jnp.maximum(m_old, jnp.max(s, axis=0, keepdims=True))
        alpha = jnp.exp2(m_old - m_new)
        p = jnp.exp2(s - m_new)
        m_ref[:, cols] = m_new
        pt_ref[:, cols] = p.astype(BF16)
        return alpha

    def fold_pending(t, half, cols, alpha_a, alpha_b):
        pv_a = weighted(jnp.maximum(t - 2, 0), pt[2 * half], cols)
        pv_b = weighted(jnp.maximum(t - 1, 0), pt[2 * half + 1], cols)
        acc = acc_ref[:, cols] + alpha_ref[:, cols] * pv_a + pv_b
        return (alpha_a * alpha_b) * acc

    def pair(t, parity):
        cur, nxt = parity, 1 - parity
        for cols in groups:
            scores(t + 2, st[2 * nxt], cols)
            scores(t + 3, st[2 * nxt + 1], cols)
            alpha_a = softmax(st[2 * cur], pt[2 * cur], cols)
            alpha_b = softmax(st[2 * cur + 1], pt[2 * cur + 1], cols)
            acc_ref[:, cols] = fold_pending(t, nxt, cols, alpha_a, alpha_b)
            alpha_ref[:, cols] = alpha_b

    def diagonal(t, parity):
        cur, nxt = parity, 1 - parity
        for cols in groups:
            alpha_a = softmax(st[2 * cur], pt[2 * cur], cols, key_offset=0)
            alpha_b = softmax(st[2 * cur + 1], pt[2 * cur + 1], cols, key_offset=tk)
            acc = fold_pending(t, nxt, cols, alpha_a, alpha_b)
            acc_ref[:, cols] = (acc + alpha_b * weighted(t, pt[2 * cur], cols)
                                + weighted(t + 1, pt[2 * cur + 1], cols))

    m_ref[...] = jnp.full(m_ref.shape, -jnp.inf, F32)
    acc_ref[...] = jnp.zeros(acc_ref.shape, F32)
    alpha_ref[...] = jnp.ones(alpha_ref.shape, F32)
    pt[2][...] = jnp.zeros(pt[2].shape, BF16)
    pt[3][...] = jnp.zeros(pt[3].shape, BF16)
    for cols in groups:
        scores(0, st[0], cols)
        scores(1, st[1], cols)

    def body(u, carry):
        for parity in (0, 1):
            @pl.when(u % 2 == parity)
            def _():
                pair(2 * u, parity)
        return carry

    lax.fori_loop(0, qi, body, 0)
    for parity in (0, 1):
        @pl.when(qi % 2 == parity)
        def _():
            diagonal(2 * qi, parity)

    ot = acc_ref[:VALUE_DIM, :] / acc_ref[VALUE_DIM:VALUE_DIM + 1, :]
    lam = _diff_lambda(lq1_ref, lk1_ref, lq2_ref, lk2_ref)
    o = ot[:, :tq] - lam * ot[:, tq:]
    y = (o * lax.rsqrt(jnp.mean(o * o, axis=0, keepdims=True) + RMS_EPS) * subln_t_ref[...]
         * (1.0 - LAMBDA_INIT))
    o_ref[...] = y.T.astype(BF16)


def _attn_sample_body(c, n_chunks, q_ref, kn_ref, vn_ref, lq1_ref, lk1_ref, lq2_ref, lk2_ref,
                      subln_ref, k_pages, v_pages, o_ref, m_ref, l_ref, acc_ref, kb_ref, vb_ref):
    npg = len(k_pages)
    nt = q_ref.shape[0]
    nrow = SAMPLE_ROWS

    q = q_ref[...]
    qrows = jnp.concatenate([q] * (nrow // nt), axis=0)
    dim = lax.broadcasted_iota(jnp.int32, qrows.shape, 1)
    row = lax.broadcasted_iota(jnp.int32, qrows.shape, 0)
    live = (dim // HEAD_DIM == row // nt) & (row < 2 * nt * N_HEADS)
    qm_f32 = jnp.where(live, qrows, 0.0)
    qm = qm_f32.astype(BF16)
    contract_last = (((1,), (1,)), ((), ()))

    @pl.when(c == 0)
    def _():
        head_rows = lambda ref, hd: ref[pl.ds(hd, nt, stride=N_HEADS), :]
        kn = jnp.concatenate([head_rows(kn_ref, hd) for hd in range(N_HEADS)], axis=1)
        vn = jnp.concatenate([head_rows(vn_ref, hd) for hd in range(N_HEADS)], axis=1)
        s = lax.dot_general(qm_f32, kn, contract_last, preferred_element_type=F32)
        qry_t = lax.broadcasted_iota(jnp.int32, s.shape, 0) % nt
        key_t = lax.broadcasted_iota(jnp.int32, s.shape, 1)
        s = jnp.where(key_t <= qry_t, s, -jnp.inf)
        m = jnp.max(s, axis=1, keepdims=True)
        p = jnp.exp2(s - m)
        m_ref[...] = m
        l_ref[...] = jnp.sum(p, axis=1, keepdims=True)
        acc_ref[...] = jnp.dot(p, vn, preferred_element_type=F32)

    def relayout(pages, buf_ref, part):
        for i in range(part * pages_per_part, (part + 1) * pages_per_part):
            rows = slice(i * PAGE_SIZE, (i + 1) * PAGE_SIZE)
            for hd in range(N_HEADS):
                lanes = slice(hd * KEY_DIM, (hd + 1) * KEY_DIM)
                tokens = pl.ds(hd, PAGE_SIZE, stride=N_HEADS)
                buf_ref[rows, lanes] = pages[i][tokens, :].astype(BF16)

    pages_per_part = npg // SAMPLE_PARTS
    part_rows = [slice(i * pages_per_part * PAGE_SIZE, (i + 1) * pages_per_part * PAGE_SIZE)
                 for i in range(SAMPLE_PARTS)]
    scores = []
    for part, rows in enumerate(part_rows):
        relayout(k_pages, kb_ref, part)
        scores.append(lax.dot_general(qm, kb_ref[rows, :], contract_last,
                                      preferred_element_type=F32))
    m_old = m_ref[...]
    m_new = m_old
    for s in scores:
        m_new = jnp.maximum(m_new, jnp.max(s, axis=1, keepdims=True))
    alpha = jnp.exp2(m_old - m_new)
    m_ref[...] = m_new
    l = alpha * l_ref[...]
    acc = alpha * acc_ref[...]
    for part, rows in enumerate(part_rows):
        relayout(v_pages, vb_ref, part)
        p = jnp.exp2(scores[part] - m_new)
        l = l + jnp.sum(p, axis=1, keepdims=True)
        acc = acc + jnp.dot(p.astype(BF16), vb_ref[rows, :], preferred_element_type=F32)
    l_ref[...] = l
    acc_ref[...] = acc

    @pl.when(c == n_chunks - 1)
    def _():
        o = acc_ref[...] / l_ref[...]
        lam = _diff_lambda(lq1_ref, lk1_ref, lq2_ref, lk2_ref)
        for h in range(N_HEADS):
            r0 = h * 2 * nt
            o1 = o[r0:r0 + nt, h * VALUE_DIM:(h + 1) * VALUE_DIM]
            o2 = o[r0 + nt:r0 + 2 * nt, h * VALUE_DIM:(h + 1) * VALUE_DIM]
            o_ref[:, h * VALUE_DIM:(h + 1) * VALUE_DIM] = _sub_norm(
                o1, o2, lam, subln_ref[...])


N_PROMPT_SCRATCH = 3 + 8


def _attn_kernel(pt_ref, qt_ref, k_ref, vt_ref, lq1_ref, lk1_ref, lq2_ref, lk2_ref, subln_t_ref,
                 qs_ref, kn_ref, vn_ref, subln_ref, *rest, n_chunks):
    del pt_ref
    npg = PAGES_PER_STEP
    k_pages, v_pages = rest[:npg], rest[npg:2 * npg]
    o_ref, os_ref = rest[2 * npg:2 * npg + 2]
    scratch = rest[2 * npg + 2:]
    qi = pl.program_id(2)
    lam_refs = (lq1_ref, lk1_ref, lq2_ref, lk2_ref)
    _attn_prompt_body(qi, qt_ref, k_ref, vt_ref, *lam_refs, subln_t_ref, o_ref,
                      *scratch[:N_PROMPT_SCRATCH])
    _attn_sample_body(qi % n_chunks, n_chunks, qs_ref, kn_ref, vn_ref, *lam_refs, subln_ref,
                      k_pages, v_pages, os_ref, *scratch[N_PROMPT_SCRATCH:])


def _attention(qt, k, vt, qs, k_new, v_new, cache_k, cache_v, page_table, lq1, lk1, lq2, lk2,
               subln):
    b, _, s, _ = k.shape
    db, nt, _ = qs.shape
    n_pages = page_table.shape[1]
    tq, tk, npg = ATTN_Q_TILE, ATTN_KV_TILE, PAGES_PER_STEP
    assert s % tq == 0 and n_pages % npg == 0 and npg % SAMPLE_PARTS == 0
    assert SAMPLE_ROWS % nt == 0 and 2 * nt * N_HEADS <= SAMPLE_ROWS
    nq, n_chunks = s // tq, n_pages // npg
    assert nq % n_chunks == 0 and b * N_HEADS * (nq // n_chunks) == db
    seqs_per_head = nq // n_chunks

    def seq(bi, hi, qi):
        return (bi * N_HEADS + hi) * seqs_per_head + qi // n_chunks

    small = lambda a: pl.BlockSpec(a.shape, lambda bi, hi, qi, pt: (0, 0))
    new_rows = pl.BlockSpec((nt * N_HEADS, KEY_DIM), lambda bi, hi, qi, pt: (seq(bi, hi, qi), 0))
    per_seq = pl.BlockSpec((None, nt, Q_COLS), lambda bi, hi, qi, pt: (seq(bi, hi, qi), 0, 0))

    def page(i):
        return pl.BlockSpec(
            (None, PAGE_SIZE * N_HEADS, KEY_DIM),
            lambda bi, hi, qi, pt: (pt[seq(bi, hi, qi), (qi % n_chunks) * npg + i], 0, 0))

    subln_t = subln.reshape(VALUE_DIM, 1)
    grid_spec = pltpu.PrefetchScalarGridSpec(
        num_scalar_prefetch=1,
        grid=(b, N_HEADS, nq),
        in_specs=([pl.BlockSpec((None, None, KEY_DIM, tq), lambda bi, hi, qi, pt: (bi, hi, 0, qi)),
                   pl.BlockSpec((None, None, s, KEY_DIM), lambda bi, hi, qi, pt: (bi, hi, 0, 0)),
                   pl.BlockSpec((None, None, VALUE_DIM, s), lambda bi, hi, qi, pt: (bi, hi, 0, 0)),
                   small(lq1), small(lk1), small(lq2), small(lk2), small(subln_t),
                   per_seq, new_rows, new_rows, small(subln)]
                  + [page(i) for i in range(npg)] * 2),
        out_specs=[pl.BlockSpec((None, tq, VALUE_DIM), lambda bi, hi, qi, pt: (bi, qi, hi)),
                   per_seq],
        scratch_shapes=(
            [pltpu.VMEM((1, 2 * tq), F32),
             pltpu.VMEM((VALUE_DIM + ATTN_SUM_ROWS, 2 * tq), F32),
             pltpu.VMEM((1, 2 * tq), F32)]
            + [pltpu.VMEM((tk, 2 * tq), F32)] * 4 + [pltpu.VMEM((tk, 2 * tq), BF16)] * 4
            + [pltpu.VMEM((SAMPLE_ROWS, 1), F32), pltpu.VMEM((SAMPLE_ROWS, 1), F32),
               pltpu.VMEM((SAMPLE_ROWS, ATT_WIDTH), F32),
               pltpu.VMEM((npg * PAGE_SIZE, Q_COLS), BF16),
               pltpu.VMEM((npg * PAGE_SIZE, ATT_WIDTH), BF16)]),
    )
    return pl.pallas_call(
        functools.partial(_attn_kernel, n_chunks=n_chunks),
        grid_spec=grid_spec,
        out_shape=[jax.ShapeDtypeStruct((b, s, ATT_WIDTH), BF16),
                   jax.ShapeDtypeStruct((db, nt, ATT_WIDTH), F32)],
        compiler_params=pltpu.CompilerParams(
            dimension_semantics=("parallel", "parallel", "arbitrary"),
            vmem_limit_bytes=V7X_VMEM_LIMIT_BYTES),
        name="attention",
    )(page_table, qt, k, vt, lq1, lk1, lq2, lk2, subln_t, qs, k_new, v_new, subln,
      *([cache_k] * npg), *([cache_v] * npg))


def _post_kernel(h_ref, a_ref, gu_ref, vs_ref, ga_ref, gb_ref, wsp_ref, bsp_ref, wpa_ref, wpb_ref,
                 wo_ref, n2_ref, w1_ref, w2_ref, nf_ref, y_ref, *, seq_per_chunk):
    tm = h_ref.shape[0]
    row = lax.broadcasted_iota(jnp.int32, (CHUNK, CHUNK), 0)
    col = lax.broadcasted_iota(jnp.int32, (CHUNK, CHUNK), 1)
    seq_len = CHUNK // seq_per_chunk
    keep = (col <= row) & (row // seq_len == col // seq_len)
    bias = bsp_ref[...]
    mixed = []
    for ci in range(tm // CHUNK):
        groups = []
        for g in range(SGU_GROUPS):
            w = jnp.where(keep, wsp_ref[g], 0.0).astype(BF16)
            vg = vs_ref[ci * CHUNK:(ci + 1) * CHUNK,
                        g * SGU_GROUP_DIM:(g + 1) * SGU_GROUP_DIM].astype(BF16)
            groups.append(jnp.dot(w, vg, preferred_element_type=F32) + bias[:, g:g + 1])
        mixed.append(jnp.concatenate(groups, axis=1))
    sp = jnp.concatenate(mixed, axis=0) if len(mixed) > 1 else mixed[0]
    sgu = (gu_ref[...] * sp).astype(BF16)
    m = (ga_ref[...] * jnp.dot(a_ref[...].astype(BF16), wpa_ref[...], preferred_element_type=F32)
         + gb_ref[...] * jnp.dot(sgu, wpb_ref[...], preferred_element_type=F32))
    h = h_ref[...] + jnp.dot(m.astype(BF16), wo_ref[...], preferred_element_type=F32)
    out = _swiglu_half_step(h, n2_ref, w1_ref, w2_ref)
    y_ref[...] = _rms(out, nf_ref[...])


def _post_stage(h, a, gu, vs, ga, gb, wsp, bsp, wpa, wpb, wo, n2, w1, w2, nf, *, seq_per_chunk):
    t = h.shape[0]
    tm = TOKEN_TILE
    assert t % tm == 0 and tm % CHUNK == 0
    widths = (D_MODEL, ATT_WIDTH, SGU_WIDTH, SGU_WIDTH, D_MODEL, D_MODEL)
    consts = (wsp, bsp, wpa, wpb, wo, n2, w1, w2, nf)
    return pl.pallas_call(
        functools.partial(_post_kernel, seq_per_chunk=seq_per_chunk),
        grid=(t // tm,),
        in_specs=[_rows(tm, w) for w in widths] + [_resident(c.shape) for c in consts],
        out_specs=_rows(tm, D_MODEL),
        out_shape=jax.ShapeDtypeStruct((t, D_MODEL), F32),
        compiler_params=pltpu.CompilerParams(dimension_semantics=("parallel",),
                                             vmem_limit_bytes=V7X_VMEM_LIMIT_BYTES),
        name="post_stage",
    )(h, a, gu, vs, ga, gb, *consts)


def kernel(x_prompt, x_sample, cache_k, cache_v, page_table, norm_ffn1, ffn1_w_in, ffn1_w_out,
           norm_mix, w_in, lambda_q1, lambda_k1, lambda_q2, lambda_k2, subln, sgu_ln_g, sgu_ln_b,
           sgu_w, sgu_b, w_proj_attn, w_proj_sgu, w_out, norm_ffn2, ffn2_w_in, ffn2_w_out,
           norm_final):
    assert norm_ffn1.shape[0] == 1, "single-layer stack"
    b, s, d = x_prompt.shape
    db, nt, _ = x_sample.shape
    assert CHUNK % nt == 0 and (db * nt) % CHUNK == 0
    tail_start = ((s - 1) // CHUNK) * CHUNK

    bf = lambda w: w[0].astype(BF16)
    pre_w = (norm_ffn1, bf(ffn1_w_in), bf(ffn1_w_out), norm_mix, bf(w_in), sgu_ln_g, sgu_ln_b)
    lam_w = (lambda_q1, lambda_k1, lambda_q2, lambda_k2, subln)
    post_w = (bf(w_proj_attn), bf(w_proj_sgu), bf(w_out), norm_ffn2, bf(ffn2_w_in), bf(ffn2_w_out),
              norm_final.reshape(1, d))

    h, k, v, gu, vs, ga, gb, qh, kh, vh = _pre_stage(x_prompt.reshape(b * s, d), *pre_w,
                                                     head_major_seq=s)
    hs, ks, vsm, gus, vss, gas, gbs, qs = _pre_stage(x_sample.reshape(db * nt, d), *pre_w)
    n_pool = cache_k.shape[1]
    a, a_s = _attention(qh, kh, vh, qs.reshape(db, nt, Q_COLS), ks, vsm,
                        cache_k.reshape(n_pool, PAGE_SIZE * N_HEADS, KEY_DIM),
                        cache_v.reshape(n_pool, PAGE_SIZE * N_HEADS, VALUE_DIM), page_table, *lam_w)

    y_prompt = _post_stage(h, a.reshape(b * s, ATT_WIDTH), gu, vs, ga, gb,
                           sgu_w[0], sgu_b[0].T, *post_w, seq_per_chunk=1)
    k_prompt = k.reshape(1, b, s, N_HEADS, KEY_DIM)
    v_prompt = v.reshape(1, b, s, N_HEADS, VALUE_DIM)
    sgu_v_prompt = vs.reshape(b, s, SGU_WIDTH)[None, :, tail_start:]

    reps = CHUNK // nt
    wsp_s = jnp.tile(sgu_w[0][:, :nt, :nt], (1, reps, reps))
    bsp_s = jnp.tile(sgu_b[0][:, :nt], (1, reps)).T
    y_sample = _post_stage(hs, a_s.reshape(db * nt, ATT_WIDTH), gus, vss, gas, gbs,
                           wsp_s, bsp_s, *post_w, seq_per_chunk=reps)
    k_sample = ks.reshape(1, db, nt, N_HEADS, KEY_DIM)
    v_sample = vsm.reshape(1, db, nt, N_HEADS, VALUE_DIM)
    sgu_v_sample = vss.reshape(1, db, nt, SGU_WIDTH)

    return (y_prompt.reshape(b, s, d), y_sample.reshape(db, nt, d), k_prompt, v_prompt,
            sgu_v_prompt, k_sample, v_sample, sgu_v_sample)
```

```python
import functools
import math

import jax
import jax.numpy as jnp
from jax import lax
from jax.experimental import pallas as pl
from jax.experimental.pallas import tpu as pltpu

D_MODEL = 1024
D_FF = 2816
N_HEADS = 4
HEAD_DIM = 64
KEY_DIM = 2 * HEAD_DIM
VALUE_DIM = 2 * HEAD_DIM
Q_COLS = N_HEADS * KEY_DIM
ATT_WIDTH = N_HEADS * VALUE_DIM
SGU_WIDTH = 512
SGU_GROUPS = 4
SGU_GROUP_DIM = SGU_WIDTH // SGU_GROUPS
CHUNK = 128
PAGE_SIZE = 128
SCALE = HEAD_DIM ** -0.5
RMS_EPS = 1e-6
LN_EPS = 1e-5
LAMBDA_INIT = 0.8 - 0.6 * math.exp(-0.3 * 0)
SQRT_HALF = math.sqrt(0.5)
LOG2_E = math.log2(math.e)

_Q0, _K0, _V0, _U0, _VG0, _GA0, _GB0, _END = 0, 512, 1024, 1536, 2048, 2560, 3584, 4608

BF16 = jnp.bfloat16
F32 = jnp.float32

V7X_VMEM_LIMIT_BYTES = 60 * 1024 * 1024

TOKEN_TILE = 256
ATTN_Q_TILE = 512
ATTN_KV_TILE = 256
ATTN_COL_GROUP = 256
ATTN_SUM_ROWS = 16
PAGES_PER_STEP = 16
SAMPLE_PARTS = 2
SAMPLE_ROWS = 128


def _rms(x, g, eps=RMS_EPS):
    return x * lax.rsqrt(jnp.mean(x * x, axis=-1, keepdims=True) + eps) * g


def _gelu(x):
    return 0.5 * x * (1.0 + lax.erf(x * SQRT_HALF))


def _swiglu_half_step(x, g_ref, w_in_ref, w_out_ref):
    xn = _rms(x, g_ref[...]).astype(BF16)
    gate_up = jnp.dot(xn, w_in_ref[...], preferred_element_type=F32)
    gate = gate_up[:, :D_FF]
    up = gate_up[:, D_FF:]
    act = (gate * jax.nn.sigmoid(gate) * up).astype(BF16)
    return x + 0.5 * jnp.dot(act, w_out_ref[...], preferred_element_type=F32)


def _diff_lambda(lq1_ref, lk1_ref, lq2_ref, lk2_ref):
    s1 = jnp.sum(lq1_ref[...] * lk1_ref[...], axis=-1, keepdims=True)
    s2 = jnp.sum(lq2_ref[...] * lk2_ref[...], axis=-1, keepdims=True)
    return jnp.exp(s1) - jnp.exp(s2) + LAMBDA_INIT


def _sub_norm(o1, o2, lam, subln):
    o = o1 - lam * o2
    return _rms(o, subln) * (1.0 - LAMBDA_INIT)


def _pre_kernel(x_ref, n1_ref, w1_ref, w2_ref, nm_ref, win_ref, lng_ref, lnb_ref,
                h_ref, k_ref, v_ref, gu_ref, vs_ref, ga_ref, gb_ref, *attn_refs):
    h = _swiglu_half_step(x_ref[...], n1_ref, w1_ref, w2_ref)
    h_ref[...] = h
    nrm = _rms(h, nm_ref[...]).astype(BF16)

    def proj(lo, hi):
        return jnp.dot(nrm, win_ref[:, lo:hi], preferred_element_type=F32)

    q = proj(_Q0, _K0) * (SCALE * LOG2_E)
    k = proj(_K0, _V0)
    v = proj(_V0, _U0)
    head = lambda z, hd: z[:, hd * KEY_DIM:(hd + 1) * KEY_DIM]
    tm = x_ref.shape[0]
    for hd in range(N_HEADS):
        k_ref[pl.ds(hd, tm, stride=N_HEADS), :] = head(k, hd)
        v_ref[pl.ds(hd, tm, stride=N_HEADS), :] = head(v, hd)
    if len(attn_refs) == 3:
        qt_ref, kh_ref, vt_ref = attn_refs
        for hd in range(N_HEADS):
            qt_ref[hd] = head(q, hd).T.astype(BF16)
            kh_ref[hd] = head(k, hd).astype(BF16)
            vt_ref[hd] = head(v, hd).T.astype(BF16)
    else:
        attn_refs[0][...] = q
    gu_ref[...] = _gelu(proj(_U0, _VG0))
    gv = _gelu(proj(_VG0, _GA0))
    mu = jnp.mean(gv, axis=-1, keepdims=True)
    gc = gv - mu
    vs_ref[...] = (gc * lax.rsqrt(jnp.mean(gc * gc, axis=-1, keepdims=True) + LN_EPS)
                   * lng_ref[...] + lnb_ref[...])
    ga_ref[...] = jax.nn.sigmoid(proj(_GA0, _GB0))
    gb_ref[...] = jax.nn.sigmoid(proj(_GB0, _END))


def _resident(shape):
    return pl.BlockSpec(shape, lambda *_: (0,) * len(shape), pipeline_mode=pl.Buffered(1))


def _rows(tm, width):
    return pl.BlockSpec((tm, width), lambda i: (i, 0))


def _pre_stage(x, n1, w1, w2, nm, win, lng, lnb, *, head_major_seq=None):
    t = x.shape[0]
    tm = TOKEN_TILE
    assert t % tm == 0
    cache_rows = _rows(tm * N_HEADS, KEY_DIM)
    cache_shape = jax.ShapeDtypeStruct((t * N_HEADS, KEY_DIM), F32)
    flat = lambda w: (_rows(tm, w), jax.ShapeDtypeStruct((t, w), F32))
    outs = [flat(D_MODEL), (cache_rows, cache_shape), (cache_rows, cache_shape), flat(SGU_WIDTH),
            flat(SGU_WIDTH), flat(D_MODEL), flat(D_MODEL)]
    if head_major_seq is not None:
        seq = head_major_seq
        assert seq % tm == 0 and t % seq == 0
        per = seq // tm
        rows = (pl.BlockSpec((None, N_HEADS, tm, KEY_DIM), lambda i: (i // per, 0, i % per, 0)),
                jax.ShapeDtypeStruct((t // seq, N_HEADS, seq, KEY_DIM), BF16))
        cols = (pl.BlockSpec((None, N_HEADS, KEY_DIM, tm), lambda i: (i // per, 0, 0, i % per)),
                jax.ShapeDtypeStruct((t // seq, N_HEADS, KEY_DIM, seq), BF16))
        outs += [cols, rows, cols]
    else:
        outs += [flat(Q_COLS)]
    return pl.pallas_call(
        _pre_kernel,
        grid=(t // tm,),
        in_specs=[_rows(tm, D_MODEL), _resident(n1.shape), _resident(w1.shape), _resident(w2.shape),
                  _resident(nm.shape), _resident(win.shape), _resident(lng.shape), _resident(lnb.shape)],
        out_specs=[o[0] for o in outs],
        out_shape=[o[1] for o in outs],
        compiler_params=pltpu.CompilerParams(dimension_semantics=("parallel",),
                                             vmem_limit_bytes=V7X_VMEM_LIMIT_BYTES),
        name="pre_stage",
    )(x, n1, w1, w2, nm, win, lng, lnb)


def _attn_prompt_body(qi, qt_ref, k_ref, vt_ref, lq1_ref, lk1_ref, lq2_ref, lk2_ref, subln_t_ref,
                      o_ref, m_ref, acc_ref, alpha_ref, *bufs):
    tq, tk = ATTN_Q_TILE, ATTN_KV_TILE
    assert tq == 2 * tk
    st, pt = bufs[:4], bufs[4:]

    qt = qt_ref[...]
    dim = lax.broadcasted_iota(jnp.int32, qt.shape, 0)
    zero = jnp.zeros_like(qt)
    qqt = jnp.concatenate([jnp.where(dim < HEAD_DIM, qt, zero),
                           jnp.where(dim >= HEAD_DIM, qt, zero)], axis=1)

    groups = [slice(c, c + ATTN_COL_GROUP) for c in range(0, 2 * tq, ATTN_COL_GROUP)]

    def scores(blk, st_lo, st_hi, cols):
        start = pl.multiple_of(blk * tk, 2 * tk)
        s2 = jnp.dot(k_ref[pl.ds(start, 2 * tk), :], qqt[:, cols], preferred_element_type=F32)
        st_lo[:, cols] = s2[:tk]
        st_hi[:, cols] = s2[tk:]

    ones_rows = jnp.ones((ATTN_SUM_ROWS, tk), BF16)

    def weighted(blk, pt_ref, cols):
        start = pl.multiple_of(blk * tk, tk)
        lhs = jnp.concatenate([vt_ref[:, pl.ds(start, tk)], ones_rows], axis=0)
        return jnp.dot(lhs, pt_ref[:, cols], preferred_element_type=F32)

    def softmax(st_ref, pt_ref, cols, key_offset=None):
        s = st_ref[:, cols]
        if key_offset is not None:
            key = lax.broadcasted_iota(jnp.int32, s.shape, 0) + key_offset
            qpos = lax.broadcasted_iota(jnp.int32, s.shape, 1) + cols.start % tq
            s = jnp.where(key <= qpos, s, -jnp.inf)
        m_old = m_ref[:, cols]
        m_new = jnp.maximum(m_old, jnp.max(s, axis=0, keepdims=True))
        alpha = jnp.exp2(m_old - m_new)
        p = jnp.exp2(s - m_new)
        m_ref[:, cols] = m_new
        pt_ref[:, cols] = p.astype(BF16)
        return alpha

    def fold_pending(t, half, cols, alpha_a, alpha_b):
        pv_a = weighted(jnp.maximum(t - 2, 0), pt[2 * half], cols)
        pv_b = weighted(jnp.maximum(t - 1, 0), pt[2 * half + 1], cols)
        acc = acc_ref[:, cols] + alpha_ref[:, cols] * pv_a + pv_b
        return (alpha_a if alpha_b is None else alpha_a * alpha_b) * acc

    def pair(t, parity):
        cur, nxt = parity, 1 - parity
        for cols in groups:
            scores(t + 2, st[2 * nxt], st[2 * nxt + 1], cols)
            alpha_a = softmax(st[2 * cur], pt[2 * cur], cols)
            alpha_b = softmax(st[2 * cur + 1], pt[2 * cur + 1], cols)
            acc_ref[:, cols] = fold_pending(t, nxt, cols, alpha_a, alpha_b)
            alpha_ref[:, cols] = alpha_b

    def diagonal(t, parity):
        cur, nxt = parity, 1 - parity
        for cols in groups:
            first_q = cols.start % tq
            sees_all_lo = first_q >= tk - 1
            sees_no_hi = first_q + ATTN_COL_GROUP <= tk
            alpha_a = softmax(st[2 * cur], pt[2 * cur], cols,
                              key_offset=None if sees_all_lo else 0)
            if sees_no_hi:
                acc = fold_pending(t, nxt, cols, alpha_a, None)
                acc_ref[:, cols] = acc + weighted(t, pt[2 * cur], cols)
            else:
                alpha_b = softmax(st[2 * cur + 1], pt[2 * cur + 1], cols, key_offset=tk)
                acc = fold_pending(t, nxt, cols, alpha_a, alpha_b)
                acc_ref[:, cols] = (acc + alpha_b * weighted(t, pt[2 * cur], cols)
                                    + weighted(t + 1, pt[2 * cur + 1], cols))

    m_ref[...] = jnp.full(m_ref.shape, -jnp.inf, F32)
    acc_ref[...] = jnp.zeros(acc_ref.shape, F32)
    alpha_ref[...] = jnp.ones(alpha_ref.shape, F32)
    pt[2][...] = jnp.zeros(pt[2].shape, BF16)
    pt[3][...] = jnp.zeros(pt[3].shape, BF16)
    for cols in groups:
        scores(0, st[0], st[1], cols)

    def body(u, carry):
        for parity in (0, 1):
            @pl.when(u % 2 == parity)
            def _():
                pair(2 * u, parity)
        return carry

    lax.fori_loop(0, qi, body, 0)
    for parity in (0, 1):
        @pl.when(qi % 2 == parity)
        def _():
            diagonal(2 * qi, parity)

    ot = acc_ref[:VALUE_DIM, :] / acc_ref[VALUE_DIM:VALUE_DIM + 1, :]
    lam = _diff_lambda(lq1_ref, lk1_ref, lq2_ref, lk2_ref)
    o = ot[:, :tq] - lam * ot[:, tq:]
    y = (o * lax.rsqrt(jnp.mean(o * o, axis=0, keepdims=True) + RMS_EPS) * subln_t_ref[...]
         * (1.0 - LAMBDA_INIT))
    o_ref[...] = y.T.astype(BF16)


def _attn_sample_body(c, n_chunks, q_ref, kn_ref, vn_ref, lq1_ref, lk1_ref, lq2_ref, lk2_ref,
                      subln_ref, k_pages, v_pages, o_ref, m_ref, l_ref, acc_ref, kb_ref, vb_ref):
    npg = len(k_pages)
    nt = q_ref.shape[0]
    nrow = SAMPLE_ROWS

    q = q_ref[...]
    qrows = jnp.concatenate([q] * (nrow // nt), axis=0)
    dim = lax.broadcasted_iota(jnp.int32, qrows.shape, 1)
    row = lax.broadcasted_iota(jnp.int32, qrows.shape, 0)
    live = (dim // HEAD_DIM == row // nt) & (row < 2 * nt * N_HEADS)
    qm_f32 = jnp.where(live, qrows, 0.0)
    qm = qm_f32.astype(BF16)
    contract_last = (((1,), (1,)), ((), ()))

    @pl.when(c == 0)
    def _():
        head_rows = lambda ref, hd: ref[pl.ds(hd, nt, stride=N_HEADS), :]
        kn = jnp.concatenate([head_rows(kn_ref, hd) for hd in range(N_HEADS)], axis=1)
        vn = jnp.concatenate([head_rows(vn_ref, hd) for hd in range(N_HEADS)], axis=1)
        s = lax.dot_general(qm_f32, kn, contract_last, preferred_element_type=F32)
        qry_t = lax.broadcasted_iota(jnp.int32, s.shape, 0) % nt
        key_t = lax.broadcasted_iota(jnp.int32, s.shape, 1)
        s = jnp.where(key_t <= qry_t, s, -jnp.inf)
        m = jnp.max(s, axis=1, keepdims=True)
        p = jnp.exp2(s - m)
        m_ref[...] = m
        l_ref[...] = jnp.sum(p, axis=1, keepdims=True)
        acc_ref[...] = jnp.dot(p, vn, preferred_element_type=F32)

    def relayout(pages, buf_ref, part):
        for i in range(part * pages_per_part, (part + 1) * pages_per_part):
            rows = slice(i * PAGE_SIZE, (i + 1) * PAGE_SIZE)
            for hd in range(N_HEADS):
                lanes = slice(hd * KEY_DIM, (hd + 1) * KEY_DIM)
                tokens = pl.ds(hd, PAGE_SIZE, stride=N_HEADS)
                buf_ref[rows, lanes] = pages[i][tokens, :].astype(BF16)

    pages_per_part = npg // SAMPLE_PARTS
    part_rows = [slice(i * pages_per_part * PAGE_SIZE, (i + 1) * pages_per_part * PAGE_SIZE)
                 for i in range(SAMPLE_PARTS)]
    scores = []
    for part, rows in enumerate(part_rows):
        relayout(k_pages, kb_ref, part)
        scores.append(lax.dot_general(qm, kb_ref[rows, :], contract_last,
                                      preferred_element_type=F32))
    m_old = m_ref[...]
    m_new = m_old
    for s in scores:
        m_new = jnp.maximum(m_new, jnp.max(s, axis=1, keepdims=True))
    alpha = jnp.exp2(m_old - m_new)
    m_ref[...] = m_new
    l = alpha * l_ref[...]
    acc = alpha * acc_ref[...]
    for part, rows in enumerate(part_rows):
        relayout(v_pages, vb_ref, part)
        p = jnp.exp2(scores[part] - m_new)
        l = l + jnp.sum(p, axis=1, keepdims=True)
        acc = acc + jnp.dot(p.astype(BF16), vb_ref[rows, :], preferred_element_type=F32)
    l_ref[...] = l
    acc_ref[...] = acc

    @pl.when(c == n_chunks - 1)
    def _():
        o = acc_ref[...] / l_ref[...]
        lam = _diff_lambda(lq1_ref, lk1_ref, lq2_ref, lk2_ref)
        for h in range(N_HEADS):
            r0 = h * 2 * nt
            o1 = o[r0:r0 + nt, h * VALUE_DIM:(h + 1) * VALUE_DIM]
            o2 = o[r0 + nt:r0 + 2 * nt, h * VALUE_DIM:(h + 1) * VALUE_DIM]
            o_ref[:, h * VALUE_DIM:(h + 1) * VALUE_DIM] = _sub_norm(
                o1, o2, lam, subln_ref[...])


N_PROMPT_SCRATCH = 3 + 8


def _attn_kernel(pt_ref, qt_ref, k_ref, vt_ref, lq1_ref, lk1_ref, lq2_ref, lk2_ref, subln_t_ref,
                 qs_ref, kn_ref, vn_ref, subln_ref, *rest, n_chunks):
    del pt_ref
    npg = PAGES_PER_STEP
    k_pages, v_pages = rest[:npg], rest[npg:2 * npg]
    o_ref, os_ref = rest[2 * npg:2 * npg + 2]
    scratch = rest[2 * npg + 2:]
    qi = pl.program_id(2)
    lam_refs = (lq1_ref, lk1_ref, lq2_ref, lk2_ref)
    _attn_prompt_body(qi, qt_ref, k_ref, vt_ref, *lam_refs, subln_t_ref, o_ref,
                      *scratch[:N_PROMPT_SCRATCH])
    _attn_sample_body(qi % n_chunks, n_chunks, qs_ref, kn_ref, vn_ref, *lam_refs, subln_ref,
                      k_pages, v_pages, os_ref, *scratch[N_PROMPT_SCRATCH:])


def _attention(qt, k, vt, qs, k_new, v_new, cache_k, cache_v, page_table, lq1, lk1, lq2, lk2,
               subln):
    b, _, s, _ = k.shape
    db, nt, _ = qs.shape
    n_pages = page_table.shape[1]
    tq, tk, npg = ATTN_Q_TILE, ATTN_KV_TILE, PAGES_PER_STEP
    assert s % tq == 0 and n_pages % npg == 0 and npg % SAMPLE_PARTS == 0
    assert SAMPLE_ROWS % nt == 0 and 2 * nt * N_HEADS <= SAMPLE_ROWS
    nq, n_chunks = s // tq, n_pages // npg
    assert nq % n_chunks == 0 and b * N_HEADS * (nq // n_chunks) == db
    seqs_per_head = nq // n_chunks

    def seq(bi, hi, qi):
        return (bi * N_HEADS + hi) * seqs_per_head + qi // n_chunks

    small = lambda a: pl.BlockSpec(a.shape, lambda bi, hi, qi, pt: (0, 0))
    new_rows = pl.BlockSpec((nt * N_HEADS, KEY_DIM), lambda bi, hi, qi, pt: (seq(bi, hi, qi), 0))
    per_seq = pl.BlockSpec((None, nt, Q_COLS), lambda bi, hi, qi, pt: (seq(bi, hi, qi), 0, 0))

    def page(i):
        return pl.BlockSpec(
            (None, PAGE_SIZE * N_HEADS, KEY_DIM),
            lambda bi, hi, qi, pt: (pt[seq(bi, hi, qi), (qi % n_chunks) * npg + i], 0, 0))

    subln_t = subln.reshape(VALUE_DIM, 1)
    grid_spec = pltpu.PrefetchScalarGridSpec(
        num_scalar_prefetch=1,
        grid=(b, N_HEADS, nq),
        in_specs=([pl.BlockSpec((None, None, KEY_DIM, tq), lambda bi, hi, qi, pt: (bi, hi, 0, qi)),
                   pl.BlockSpec((None, None, s, KEY_DIM), lambda bi, hi, qi, pt: (bi, hi, 0, 0)),
                   pl.BlockSpec((None, None, VALUE_DIM, s), lambda bi, hi, qi, pt: (bi, hi, 0, 0)),
                   small(lq1), small(lk1), small(lq2), small(lk2), small(subln_t),
                   per_seq, new_rows, new_rows, small(subln)]
                  + [page(i) for i in range(npg)] * 2),
        out_specs=[pl.BlockSpec((None, tq, VALUE_DIM), lambda bi, hi, qi, pt: (bi, qi, hi)),
                   per_seq],
        scratch_shapes=(
            [pltpu.VMEM((1, 2 * tq), F32),
             pltpu.VMEM((VALUE_DIM + ATTN_SUM_ROWS, 2 * tq), F32),
             pltpu.VMEM((1, 2 * tq), F32)]
            + [pltpu.VMEM((tk, 2 * tq), F32)] * 4 + [pltpu.VMEM((tk, 2 * tq), BF16)] * 4
            + [pltpu.VMEM((SAMPLE_ROWS, 1), F32), pltpu.VMEM((SAMPLE_ROWS, 1), F32),
               pltpu.VMEM((SAMPLE_ROWS, ATT_WIDTH), F32),
               pltpu.VMEM((npg * PAGE_SIZE, Q_COLS), BF16),
               pltpu.VMEM((npg * PAGE_SIZE, ATT_WIDTH), BF16)]),
    )
    return pl.pallas_call(
        functools.partial(_attn_kernel, n_chunks=n_chunks),
        grid_spec=grid_spec,
        out_shape=[jax.ShapeDtypeStruct((b, s, ATT_WIDTH), BF16),
                   jax.ShapeDtypeStruct((db, nt, ATT_WIDTH), F32)],
        compiler_params=pltpu.CompilerParams(
            dimension_semantics=("parallel", "parallel", "arbitrary"),
            vmem_limit_bytes=V7X_VMEM_LIMIT_BYTES),
        name="attention",
    )(page_table, qt, k, vt, lq1, lk1, lq2, lk2, subln_t, qs, k_new, v_new, subln,
      *([cache_k] * npg), *([cache_v] * npg))


def _post_kernel(h_ref, a_ref, gu_ref, vs_ref, ga_ref, gb_ref, wsp_ref, bsp_ref, wpa_ref, wpb_ref,
                 wo_ref, n2_ref, w1_ref, w2_ref, nf_ref, y_ref, *, seq_per_chunk):
    tm = h_ref.shape[0]
    row = lax.broadcasted_iota(jnp.int32, (CHUNK, CHUNK), 0)
    col = lax.broadcasted_iota(jnp.int32, (CHUNK, CHUNK), 1)
    seq_len = CHUNK // seq_per_chunk
    keep = (col <= row) & (row // seq_len == col // seq_len)
    bias = bsp_ref[...]
    n_chunks = tm // CHUNK
    mixed = [[None] * SGU_GROUPS for _ in range(n_chunks)]
    for g in range(SGU_GROUPS):
        lanes = slice(g * SGU_GROUP_DIM, (g + 1) * SGU_GROUP_DIM)
        w = jnp.where(keep, wsp_ref[g], 0.0).astype(BF16)
        vg = jnp.concatenate([vs_ref[ci * CHUNK:(ci + 1) * CHUNK, lanes].astype(BF16)
                              for ci in range(n_chunks)], axis=1)
        out = jnp.dot(w, vg, preferred_element_type=F32) + bias[:, g:g + 1]
        for ci in range(n_chunks):
            mixed[ci][g] = out[:, ci * SGU_GROUP_DIM:(ci + 1) * SGU_GROUP_DIM]
    sp = jnp.concatenate([jnp.concatenate(row, axis=1) for row in mixed], axis=0)
    sgu = (gu_ref[...] * sp).astype(BF16)
    m = (ga_ref[...] * jnp.dot(a_ref[...].astype(BF16), wpa_ref[...], preferred_element_type=F32)
         + gb_ref[...] * jnp.dot(sgu, wpb_ref[...], preferred_element_type=F32))
    h = h_ref[...] + jnp.dot(m.astype(BF16), wo_ref[...], preferred_element_type=F32)
    out = _swiglu_half_step(h, n2_ref, w1_ref, w2_ref)
    y_ref[...] = _rms(out, nf_ref[...])


def _post_stage(h, a, gu, vs, ga, gb, wsp, bsp, wpa, wpb, wo, n2, w1, w2, nf, *, seq_per_chunk):
    t = h.shape[0]
    tm = TOKEN_TILE
    assert t % tm == 0 and tm % CHUNK == 0
    widths = (D_MODEL, ATT_WIDTH, SGU_WIDTH, SGU_WIDTH, D_MODEL, D_MODEL)
    consts = (wsp, bsp, wpa, wpb, wo, n2, w1, w2, nf)
    return pl.pallas_call(
        functools.partial(_post_kernel, seq_per_chunk=seq_per_chunk),
        grid=(t // tm,),
        in_specs=[_rows(tm, w) for w in widths] + [_resident(c.shape) for c in consts],
        out_specs=_rows(tm, D_MODEL),
        out_shape=jax.ShapeDtypeStruct((t, D_MODEL), F32),
        compiler_params=pltpu.CompilerParams(dimension_semantics=("parallel",),
                                             vmem_limit_bytes=V7X_VMEM_LIMIT_BYTES),
        name="post_stage",
    )(h, a, gu, vs, ga, gb, *consts)


def kernel(x_prompt, x_sample, cache_k, cache_v, page_table, norm_ffn1, ffn1_w_in, ffn1_w_out,
           norm_mix, w_in, lambda_q1, lambda_k1, lambda_q2, lambda_k2, subln, sgu_ln_g, sgu_ln_b,
           sgu_w, sgu_b, w_proj_attn, w_proj_sgu, w_out, norm_ffn2, ffn2_w_in, ffn2_w_out,
           norm_final):
    assert norm_ffn1.shape[0] == 1, "single-layer stack"
    b, s, d = x_prompt.shape
    db, nt, _ = x_sample.shape
    assert CHUNK % nt == 0 and (db * nt) % CHUNK == 0
    tail_start = ((s - 1) // CHUNK) * CHUNK

    bf = lambda w: w[0].astype(BF16)
    pre_w = (norm_ffn1, bf(ffn1_w_in), bf(ffn1_w_out), norm_mix, bf(w_in), sgu_ln_g, sgu_ln_b)
    lam_w = (lambda_q1, lambda_k1, lambda_q2, lambda_k2, subln)
    post_w = (bf(w_proj_attn), bf(w_proj_sgu), bf(w_out), norm_ffn2, bf(ffn2_w_in), bf(ffn2_w_out),
              norm_final.reshape(1, d))

    h, k, v, gu, vs, ga, gb, qh, kh, vh = _pre_stage(x_prompt.reshape(b * s, d), *pre_w,
                                                     head_major_seq=s)
    hs, ks, vsm, gus, vss, gas, gbs, qs = _pre_stage(x_sample.reshape(db * nt, d), *pre_w)
    n_pool = cache_k.shape[1]
    a, a_s = _attention(qh, kh, vh, qs.reshape(db, nt, Q_COLS), ks, vsm,
                        cache_k.reshape(n_pool, PAGE_SIZE * N_HEADS, KEY_DIM),
                        cache_v.reshape(n_pool, PAGE_SIZE * N_HEADS, VALUE_DIM), page_table, *lam_w)

    y_prompt = _post_stage(h, a.reshape(b * s, ATT_WIDTH), gu, vs, ga, gb,
                           sgu_w[0], sgu_b[0].T, *post_w, seq_per_chunk=1)
    k_prompt = k.reshape(1, b, s, N_HEADS, KEY_DIM)
    v_prompt = v.reshape(1, b, s, N_HEADS, VALUE_DIM)
    sgu_v_prompt = vs.reshape(b, s, SGU_WIDTH)[None, :, tail_start:]

    reps = CHUNK // nt
    wsp_s = jnp.tile(sgu_w[0][:, :nt, :nt], (1, reps, reps))
    bsp_s = jnp.tile(sgu_b[0][:, :nt], (1, reps)).T
    y_sample = _post_stage(hs, a_s.reshape(db * nt, ATT_WIDTH), gus, vss, gas, gbs,
                           wsp_s, bsp_s, *post_w, seq_per_chunk=reps)
    k_sample = ks.reshape(1, db, nt, N_HEADS, KEY_DIM)
    v_sample = vsm.reshape(1, db, nt, N_HEADS, VALUE_DIM)
    sgu_v_sample = vss.reshape(1, db, nt, SGU_WIDTH)

    return (y_prompt.reshape(b, s, d), y_sample.reshape(db, nt, d), k_prompt, v_prompt,
            sgu_v_prompt, k_sample, v_sample, sgu_v_sample)
```

```python
import functools
import math

import jax
import jax.numpy as jnp
from jax import lax
from jax.experimental import pallas as pl
from jax.experimental.pallas import tpu as pltpu

D_MODEL = 1024
D_FF = 2816
N_HEADS = 4
HEAD_DIM = 64
KEY_DIM = 2 * HEAD_DIM
VALUE_DIM = 2 * HEAD_DIM
Q_COLS = N_HEADS * KEY_DIM
ATT_WIDTH = N_HEADS * VALUE_DIM
SGU_WIDTH = 512
SGU_GROUPS = 4
SGU_GROUP_DIM = SGU_WIDTH // SGU_GROUPS
CHUNK = 128
PAGE_SIZE = 128
SCALE = HEAD_DIM ** -0.5
RMS_EPS = 1e-6
LN_EPS = 1e-5
LAMBDA_INIT = 0.8 - 0.6 * math.exp(-0.3 * 0)
SQRT_HALF = math.sqrt(0.5)
LOG2_E = math.log2(math.e)

_Q0, _K0, _V0, _U0, _VG0, _GA0, _GB0, _END = 0, 512, 1024, 1536, 2048, 2560, 3584, 4608

BF16 = jnp.bfloat16
F32 = jnp.float32

V7X_VMEM_LIMIT_BYTES = 60 * 1024 * 1024

TOKEN_TILE = 256
ATTN_Q_TILE = 512
ATTN_KV_TILE = 256
ATTN_COL_GROUP = 256
ATTN_SUM_ROWS = 16
PAGES_PER_STEP = 16
SAMPLE_PARTS = 2
SAMPLE_ROWS = 128


def _rms(x, g, eps=RMS_EPS):
    return x * lax.rsqrt(jnp.mean(x * x, axis=-1, keepdims=True) + eps) * g


def _gelu(x):
    return 0.5 * x * (1.0 + lax.erf(x * SQRT_HALF))


def _swiglu_half_step(x, g_ref, w_in_ref, w_out_ref):
    xn = _rms(x, g_ref[...]).astype(BF16)
    gate_up = jnp.dot(xn, w_in_ref[...], preferred_element_type=F32)
    gate = gate_up[:, :D_FF]
    up = gate_up[:, D_FF:]
    act = (gate * jax.nn.sigmoid(gate) * up).astype(BF16)
    return x + 0.5 * jnp.dot(act, w_out_ref[...], preferred_element_type=F32)


def _diff_lambda(lq1_ref, lk1_ref, lq2_ref, lk2_ref):
    s1 = jnp.sum(lq1_ref[...] * lk1_ref[...], axis=-1, keepdims=True)
    s2 = jnp.sum(lq2_ref[...] * lk2_ref[...], axis=-1, keepdims=True)
    return jnp.exp(s1) - jnp.exp(s2) + LAMBDA_INIT


def _sub_norm(o1, o2, lam, subln):
    o = o1 - lam * o2
    return _rms(o, subln) * (1.0 - LAMBDA_INIT)


def _pre_kernel(x_ref, n1_ref, w1_ref, w2_ref, nm_ref, win_ref, lng_ref, lnb_ref,
                h_ref, k_ref, v_ref, gu_ref, vs_ref, ga_ref, gb_ref, *attn_refs):
    h = _swiglu_half_step(x_ref[...], n1_ref, w1_ref, w2_ref)
    h_ref[...] = h
    nrm = _rms(h, nm_ref[...]).astype(BF16)

    def proj(lo, hi):
        return jnp.dot(nrm, win_ref[:, lo:hi], preferred_element_type=F32)

    q = proj(_Q0, _K0) * (SCALE * LOG2_E)
    k = proj(_K0, _V0)
    v = proj(_V0, _U0)
    head = lambda z, hd: z[:, hd * KEY_DIM:(hd + 1) * KEY_DIM]
    tm = x_ref.shape[0]
    for hd in range(N_HEADS):
        k_ref[pl.ds(hd, tm, stride=N_HEADS), :] = head(k, hd)
        v_ref[pl.ds(hd, tm, stride=N_HEADS), :] = head(v, hd)
    if len(attn_refs) == 3:
        qt_ref, kh_ref, vt_ref = attn_refs
        for hd in range(N_HEADS):
            qt_ref[hd] = head(q, hd).T.astype(BF16)
            kh_ref[hd] = head(k, hd).astype(BF16)
            vt_ref[hd] = head(v, hd).T.astype(BF16)
    else:
        attn_refs[0][...] = q
    gu_ref[...] = _gelu(proj(_U0, _VG0))
    gv = _gelu(proj(_VG0, _GA0))
    mu = jnp.mean(gv, axis=-1, keepdims=True)
    gc = gv - mu
    vs_ref[...] = (gc * lax.rsqrt(jnp.mean(gc * gc, axis=-1, keepdims=True) + LN_EPS)
                   * lng_ref[...] + lnb_ref[...])
    ga_ref[...] = jax.nn.sigmoid(proj(_GA0, _GB0))
    gb_ref[...] = jax.nn.sigmoid(proj(_GB0, _END))


def _resident(shape):
    return pl.BlockSpec(shape, lambda *_: (0,) * len(shape), pipeline_mode=pl.Buffered(1))


def _rows(tm, width):
    return pl.BlockSpec((tm, width), lambda i: (i, 0))


def _pre_stage(x, n1, w1, w2, nm, win, lng, lnb, *, head_major_seq=None):
    t = x.shape[0]
    tm = TOKEN_TILE
    assert t % tm == 0
    cache_rows = _rows(tm * N_HEADS, KEY_DIM)
    cache_shape = jax.ShapeDtypeStruct((t * N_HEADS, KEY_DIM), F32)
    flat = lambda w: (_rows(tm, w), jax.ShapeDtypeStruct((t, w), F32))
    outs = [flat(D_MODEL), (cache_rows, cache_shape), (cache_rows, cache_shape), flat(SGU_WIDTH),
            flat(SGU_WIDTH), flat(D_MODEL), flat(D_MODEL)]
    if head_major_seq is not None:
        seq = head_major_seq
        assert seq % tm == 0 and t % seq == 0
        per = seq // tm
        rows = (pl.BlockSpec((None, N_HEADS, tm, KEY_DIM), lambda i: (i // per, 0, i % per, 0)),
                jax.ShapeDtypeStruct((t // seq, N_HEADS, seq, KEY_DIM), BF16))
        cols = (pl.BlockSpec((None, N_HEADS, KEY_DIM, tm), lambda i: (i // per, 0, 0, i % per)),
                jax.ShapeDtypeStruct((t // seq, N_HEADS, KEY_DIM, seq), BF16))
        outs += [cols, rows, cols]
    else:
        outs += [flat(Q_COLS)]
    return pl.pallas_call(
        _pre_kernel,
        grid=(t // tm,),
        in_specs=[_rows(tm, D_MODEL), _resident(n1.shape), _resident(w1.shape), _resident(w2.shape),
                  _resident(nm.shape), _resident(win.shape), _resident(lng.shape), _resident(lnb.shape)],
        out_specs=[o[0] for o in outs],
        out_shape=[o[1] for o in outs],
        compiler_params=pltpu.CompilerParams(dimension_semantics=("parallel",),
                                             vmem_limit_bytes=V7X_VMEM_LIMIT_BYTES),
        name="pre_stage",
    )(x, n1, w1, w2, nm, win, lng, lnb)


def _attn_prompt_body(qi, qt_ref, k_ref, vt_ref, lq1_ref, lk1_ref, lq2_ref, lk2_ref, subln_t_ref,
                      o_ref, m_ref, acc_ref, alpha_ref, *bufs, overlap_work=None):
    tq, tk = ATTN_Q_TILE, ATTN_KV_TILE
    assert tq == 2 * tk
    st, pt, bm = bufs[:4], bufs[4:8], bufs[8:]

    qt = qt_ref[...]
    dim = lax.broadcasted_iota(jnp.int32, qt.shape, 0)
    zero = jnp.zeros_like(qt)
    qqt = jnp.concatenate([jnp.where(dim < HEAD_DIM, qt, zero),
                           jnp.where(dim >= HEAD_DIM, qt, zero)], axis=1)

    groups = [slice(c, c + ATTN_COL_GROUP) for c in range(0, 2 * tq, ATTN_COL_GROUP)]

    def scores(blk, half, cols):
        start = pl.multiple_of(blk * tk, 2 * tk)
        s2 = jnp.dot(k_ref[pl.ds(start, 2 * tk), :], qqt[:, cols], preferred_element_type=F32)
        for i, s in enumerate((s2[:tk], s2[tk:])):
            st[2 * half + i][:, cols] = s
            bm[2 * half + i][:, cols] = jnp.max(s, axis=0, keepdims=True)

    ones_rows = jnp.ones((ATTN_SUM_ROWS, tk), BF16)

    def weighted(blk, pt_ref, cols):
        start = pl.multiple_of(blk * tk, tk)
        lhs = jnp.concatenate([vt_ref[:, pl.ds(start, tk)], ones_rows], axis=0)
        return jnp.dot(lhs, pt_ref[:, cols], preferred_element_type=F32)

    def softmax(buf, cols, key_offset=None):
        s = st[buf][:, cols]
        if key_offset is None:
            s_max = bm[buf][:, cols]
        else:
            key = lax.broadcasted_iota(jnp.int32, s.shape, 0) + key_offset
            qpos = lax.broadcasted_iota(jnp.int32, s.shape, 1) + cols.start % tq
            s = jnp.where(key <= qpos, s, -jnp.inf)
            s_max = jnp.max(s, axis=0, keepdims=True)
        m_old = m_ref[:, cols]
        m_new = jnp.maximum(m_old, s_max)
        alpha = jnp.exp2(m_old - m_new)
        p = jnp.exp2(s - m_new)
        m_ref[:, cols] = m_new
        pt[buf][:, cols] = p.astype(BF16)
        return alpha

    def fold_pending(t, half, cols, alpha_a, alpha_b):
        pv_a = weighted(jnp.maximum(t - 2, 0), pt[2 * half], cols)
        pv_b = weighted(jnp.maximum(t - 1, 0), pt[2 * half + 1], cols)
        acc = acc_ref[:, cols] + alpha_ref[:, cols] * pv_a + pv_b
        return (alpha_a if alpha_b is None else alpha_a * alpha_b) * acc

    def pair(t, parity):
        cur, nxt = parity, 1 - parity
        for cols in groups:
            scores(t + 2, nxt, cols)
            alpha_a = softmax(2 * cur, cols)
            alpha_b = softmax(2 * cur + 1, cols)
            acc_ref[:, cols] = fold_pending(t, nxt, cols, alpha_a, alpha_b)
            alpha_ref[:, cols] = alpha_b

    def diagonal(t, parity):
        cur, nxt = parity, 1 - parity
        for cols in groups:
            first_q = cols.start % tq
            sees_all_lo = first_q >= tk - 1
            sees_no_hi = first_q + ATTN_COL_GROUP <= tk
            alpha_a = softmax(2 * cur, cols, key_offset=None if sees_all_lo else 0)
            if sees_no_hi:
                acc = fold_pending(t, nxt, cols, alpha_a, None)
                acc_ref[:, cols] = acc + weighted(t, pt[2 * cur], cols)
            else:
                alpha_b = softmax(2 * cur + 1, cols, key_offset=tk)
                acc = fold_pending(t, nxt, cols, alpha_a, alpha_b)
                acc_ref[:, cols] = (acc + alpha_b * weighted(t, pt[2 * cur], cols)
                                    + weighted(t + 1, pt[2 * cur + 1], cols))

    m_ref[...] = jnp.full(m_ref.shape, -jnp.inf, F32)
    acc_ref[...] = jnp.zeros(acc_ref.shape, F32)
    alpha_ref[...] = jnp.ones(alpha_ref.shape, F32)
    pt[2][...] = jnp.zeros(pt[2].shape, BF16)
    pt[3][...] = jnp.zeros(pt[3].shape, BF16)
    for cols in groups:
        scores(0, 0, cols)
    if overlap_work is not None:
        overlap_work()

    def body(u, carry):
        for parity in (0, 1):
            @pl.when(u % 2 == parity)
            def _():
                pair(2 * u, parity)
        return carry

    lax.fori_loop(0, qi, body, 0)
    for parity in (0, 1):
        @pl.when(qi % 2 == parity)
        def _():
            diagonal(2 * qi, parity)

    ot = acc_ref[:VALUE_DIM, :] / acc_ref[VALUE_DIM:VALUE_DIM + 1, :]
    lam = _diff_lambda(lq1_ref, lk1_ref, lq2_ref, lk2_ref)
    o = ot[:, :tq] - lam * ot[:, tq:]
    y = (o * lax.rsqrt(jnp.mean(o * o, axis=0, keepdims=True) + RMS_EPS) * subln_t_ref[...]
         * (1.0 - LAMBDA_INIT))
    o_ref[...] = y.T.astype(BF16)


def _attn_sample_phases(c, n_chunks, q_ref, kn_ref, vn_ref, lq1_ref, lk1_ref, lq2_ref, lk2_ref,
                        subln_ref, k_pages, v_pages, o_ref, m_ref, l_ref, acc_ref, kb_ref, vb_ref):
    npg = len(k_pages)
    nt = q_ref.shape[0]
    nrow = SAMPLE_ROWS

    q = q_ref[...]
    qrows = jnp.concatenate([q] * (nrow // nt), axis=0)
    dim = lax.broadcasted_iota(jnp.int32, qrows.shape, 1)
    row = lax.broadcasted_iota(jnp.int32, qrows.shape, 0)
    live = (dim // HEAD_DIM == row // nt) & (row < 2 * nt * N_HEADS)
    qm_f32 = jnp.where(live, qrows, 0.0)
    qm = qm_f32.astype(BF16)
    contract_last = (((1,), (1,)), ((), ()))

    def start():
        @pl.when(c == 0)
        def _():
            head_rows = lambda ref, hd: ref[pl.ds(hd, nt, stride=N_HEADS), :]
            kn = jnp.concatenate([head_rows(kn_ref, hd) for hd in range(N_HEADS)], axis=1)
            vn = jnp.concatenate([head_rows(vn_ref, hd) for hd in range(N_HEADS)], axis=1)
            s = lax.dot_general(qm_f32, kn, contract_last, preferred_element_type=F32)
            qry_t = lax.broadcasted_iota(jnp.int32, s.shape, 0) % nt
            key_t = lax.broadcasted_iota(jnp.int32, s.shape, 1)
            s = jnp.where(key_t <= qry_t, s, -jnp.inf)
            m = jnp.max(s, axis=1, keepdims=True)
            p = jnp.exp2(s - m)
            m_ref[...] = m
            l_ref[...] = jnp.sum(p, axis=1, keepdims=True)
            acc_ref[...] = jnp.dot(p, vn, preferred_element_type=F32)

    pages_per_part = npg // SAMPLE_PARTS
    part_rows = [slice(i * pages_per_part * PAGE_SIZE, (i + 1) * pages_per_part * PAGE_SIZE)
                 for i in range(SAMPLE_PARTS)]

    def relayout(pages, buf_ref, part):
        for i in range(part * pages_per_part, (part + 1) * pages_per_part):
            rows = slice(i * PAGE_SIZE, (i + 1) * PAGE_SIZE)
            for hd in range(N_HEADS):
                lanes = slice(hd * KEY_DIM, (hd + 1) * KEY_DIM)
                tokens = pl.ds(hd, PAGE_SIZE, stride=N_HEADS)
                buf_ref[rows, lanes] = pages[i][tokens, :].astype(BF16)

    def accumulate():
        scores = []
        for part, rows in enumerate(part_rows):
            relayout(k_pages, kb_ref, part)
            scores.append(lax.dot_general(qm, kb_ref[rows, :], contract_last,
                                          preferred_element_type=F32))
        m_old = m_ref[...]
        m_new = m_old
        for s in scores:
            m_new = jnp.maximum(m_new, jnp.max(s, axis=1, keepdims=True))
        alpha = jnp.exp2(m_old - m_new)
        m_ref[...] = m_new
        l = alpha * l_ref[...]
        acc = alpha * acc_ref[...]
        for part, rows in enumerate(part_rows):
            relayout(v_pages, vb_ref, part)
            p = jnp.exp2(scores[part] - m_new)
            l = l + jnp.sum(p, axis=1, keepdims=True)
            acc = acc + jnp.dot(p.astype(BF16), vb_ref[rows, :], preferred_element_type=F32)
        l_ref[...] = l
        acc_ref[...] = acc

    def finish():
        @pl.when(c == n_chunks - 1)
        def _():
            o = acc_ref[...] / l_ref[...]
            lam = _diff_lambda(lq1_ref, lk1_ref, lq2_ref, lk2_ref)
            for h in range(N_HEADS):
                r0 = h * 2 * nt
                o1 = o[r0:r0 + nt, h * VALUE_DIM:(h + 1) * VALUE_DIM]
                o2 = o[r0 + nt:r0 + 2 * nt, h * VALUE_DIM:(h + 1) * VALUE_DIM]
                o_ref[:, h * VALUE_DIM:(h + 1) * VALUE_DIM] = _sub_norm(
                    o1, o2, lam, subln_ref[...])

    return start, accumulate, finish


N_PROMPT_SCRATCH = 3 + 12


def _attn_kernel(pt_ref, qt_ref, k_ref, vt_ref, lq1_ref, lk1_ref, lq2_ref, lk2_ref, subln_t_ref,
                 qs_ref, kn_ref, vn_ref, subln_ref, *rest, n_chunks):
    del pt_ref
    npg = PAGES_PER_STEP
    k_pages, v_pages = rest[:npg], rest[npg:2 * npg]
    o_ref, os_ref = rest[2 * npg:2 * npg + 2]
    scratch = rest[2 * npg + 2:]
    qi = pl.program_id(2)
    lam_refs = (lq1_ref, lk1_ref, lq2_ref, lk2_ref)
    start, accumulate, finish = _attn_sample_phases(
        qi % n_chunks, n_chunks, qs_ref, kn_ref, vn_ref, *lam_refs, subln_ref,
        k_pages, v_pages, os_ref, *scratch[N_PROMPT_SCRATCH:])
    start()
    _attn_prompt_body(qi, qt_ref, k_ref, vt_ref, *lam_refs, subln_t_ref, o_ref,
                      *scratch[:N_PROMPT_SCRATCH], overlap_work=accumulate)
    finish()


def _attention(qt, k, vt, qs, k_new, v_new, cache_k, cache_v, page_table, lq1, lk1, lq2, lk2,
               subln):
    b, _, s, _ = k.shape
    db, nt, _ = qs.shape
    n_pages = page_table.shape[1]
    tq, tk, npg = ATTN_Q_TILE, ATTN_KV_TILE, PAGES_PER_STEP
    assert s % tq == 0 and n_pages % npg == 0 and npg % SAMPLE_PARTS == 0
    assert SAMPLE_ROWS % nt == 0 and 2 * nt * N_HEADS <= SAMPLE_ROWS
    nq, n_chunks = s // tq, n_pages // npg
    assert nq % n_chunks == 0 and b * N_HEADS * (nq // n_chunks) == db
    seqs_per_head = nq // n_chunks

    def seq(bi, hi, qi):
        return (bi * N_HEADS + hi) * seqs_per_head + qi // n_chunks

    small = lambda a: pl.BlockSpec(a.shape, lambda bi, hi, qi, pt: (0, 0))
    new_rows = pl.BlockSpec((nt * N_HEADS, KEY_DIM), lambda bi, hi, qi, pt: (seq(bi, hi, qi), 0))
    per_seq = pl.BlockSpec((None, nt, Q_COLS), lambda bi, hi, qi, pt: (seq(bi, hi, qi), 0, 0))

    def page(i):
        return pl.BlockSpec(
            (None, PAGE_SIZE * N_HEADS, KEY_DIM),
            lambda bi, hi, qi, pt: (pt[seq(bi, hi, qi), (qi % n_chunks) * npg + i], 0, 0))

    subln_t = subln.reshape(VALUE_DIM, 1)
    grid_spec = pltpu.PrefetchScalarGridSpec(
        num_scalar_prefetch=1,
        grid=(b, N_HEADS, nq),
        in_specs=([pl.BlockSpec((None, None, KEY_DIM, tq), lambda bi, hi, qi, pt: (bi, hi, 0, qi)),
                   pl.BlockSpec((None, None, s, KEY_DIM), lambda bi, hi, qi, pt: (bi, hi, 0, 0)),
                   pl.BlockSpec((None, None, VALUE_DIM, s), lambda bi, hi, qi, pt: (bi, hi, 0, 0)),
                   small(lq1), small(lk1), small(lq2), small(lk2), small(subln_t),
                   per_seq, new_rows, new_rows, small(subln)]
                  + [page(i) for i in range(npg)] * 2),
        out_specs=[pl.BlockSpec((None, tq, VALUE_DIM), lambda bi, hi, qi, pt: (bi, qi, hi)),
                   per_seq],
        scratch_shapes=(
            [pltpu.VMEM((1, 2 * tq), F32),
             pltpu.VMEM((VALUE_DIM + ATTN_SUM_ROWS, 2 * tq), F32),
             pltpu.VMEM((1, 2 * tq), F32)]
            + [pltpu.VMEM((tk, 2 * tq), F32)] * 4 + [pltpu.VMEM((tk, 2 * tq), BF16)] * 4
            + [pltpu.VMEM((1, 2 * tq), F32)] * 4
            + [pltpu.VMEM((SAMPLE_ROWS, 1), F32), pltpu.VMEM((SAMPLE_ROWS, 1), F32),
               pltpu.VMEM((SAMPLE_ROWS, ATT_WIDTH), F32),
               pltpu.VMEM((npg * PAGE_SIZE, Q_COLS), BF16),
               pltpu.VMEM((npg * PAGE_SIZE, ATT_WIDTH), BF16)]),
    )
    return pl.pallas_call(
        functools.partial(_attn_kernel, n_chunks=n_chunks),
        grid_spec=grid_spec,
        out_shape=[jax.ShapeDtypeStruct((b, s, ATT_WIDTH), BF16),
                   jax.ShapeDtypeStruct((db, nt, ATT_WIDTH), F32)],
        compiler_params=pltpu.CompilerParams(
            dimension_semantics=("parallel", "parallel", "arbitrary"),
            vmem_limit_bytes=V7X_VMEM_LIMIT_BYTES),
        name="attention",
    )(page_table, qt, k, vt, lq1, lk1, lq2, lk2, subln_t, qs, k_new, v_new, subln,
      *([cache_k] * npg), *([cache_v] * npg))


def _post_kernel(h_ref, a_ref, gu_ref, vs_ref, ga_ref, gb_ref, wsp_ref, bsp_ref, wpa_ref, wpb_ref,
                 wo_ref, n2_ref, w1_ref, w2_ref, nf_ref, y_ref, *, seq_per_chunk):
    tm = h_ref.shape[0]
    row = lax.broadcasted_iota(jnp.int32, (CHUNK, CHUNK), 0)
    col = lax.broadcasted_iota(jnp.int32, (CHUNK, CHUNK), 1)
    seq_len = CHUNK // seq_per_chunk
    keep = (col <= row) & (row // seq_len == col // seq_len)
    bias = bsp_ref[...]
    n_chunks = tm // CHUNK
    mixed = [[None] * SGU_GROUPS for _ in range(n_chunks)]
    for g in range(SGU_GROUPS):
        lanes = slice(g * SGU_GROUP_DIM, (g + 1) * SGU_GROUP_DIM)
        w = jnp.where(keep, wsp_ref[g], 0.0).astype(BF16)
        vg = jnp.concatenate([vs_ref[ci * CHUNK:(ci + 1) * CHUNK, lanes].astype(BF16)
                              for ci in range(n_chunks)], axis=1)
        out = jnp.dot(w, vg, preferred_element_type=F32) + bias[:, g:g + 1]
        for ci in range(n_chunks):
            mixed[ci][g] = out[:, ci * SGU_GROUP_DIM:(ci + 1) * SGU_GROUP_DIM]
    sp = jnp.concatenate([jnp.concatenate(row, axis=1) for row in mixed], axis=0)
    sgu = (gu_ref[...] * sp).astype(BF16)
    m = (ga_ref[...] * jnp.dot(a_ref[...].astype(BF16), wpa_ref[...], preferred_element_type=F32)
         + gb_ref[...] * jnp.dot(sgu, wpb_ref[...], preferred_element_type=F32))
    h = h_ref[...] + jnp.dot(m.astype(BF16), wo_ref[...], preferred_element_type=F32)
    out = _swiglu_half_step(h, n2_ref, w1_ref, w2_ref)
    y_ref[...] = _rms(out, nf_ref[...])


def _post_stage(h, a, gu, vs, ga, gb, wsp, bsp, wpa, wpb, wo, n2, w1, w2, nf, *, seq_per_chunk):
    t = h.shape[0]
    tm = TOKEN_TILE
    assert t % tm == 0 and tm % CHUNK == 0
    widths = (D_MODEL, ATT_WIDTH, SGU_WIDTH, SGU_WIDTH, D_MODEL, D_MODEL)
    consts = (wsp, bsp, wpa, wpb, wo, n2, w1, w2, nf)
    return pl.pallas_call(
        functools.partial(_post_kernel, seq_per_chunk=seq_per_chunk),
        grid=(t // tm,),
        in_specs=[_rows(tm, w) for w in widths] + [_resident(c.shape) for c in consts],
        out_specs=_rows(tm, D_MODEL),
        out_shape=jax.ShapeDtypeStruct((t, D_MODEL), F32),
        compiler_params=pltpu.CompilerParams(dimension_semantics=("parallel",),
                                             vmem_limit_bytes=V7X_VMEM_LIMIT_BYTES),
        name="post_stage",
    )(h, a, gu, vs, ga, gb, *consts)


def kernel(x_prompt, x_sample, cache_k, cache_v, page_table, norm_ffn1, ffn1_w_in, ffn1_w_out,
           norm_mix, w_in, lambda_q1, lambda_k1, lambda_q2, lambda_k2, subln, sgu_ln_g, sgu_ln_b,
           sgu_w, sgu_b, w_proj_attn, w_proj_sgu, w_out, norm_ffn2, ffn2_w_in, ffn2_w_out,
           norm_final):
    assert norm_ffn1.shape[0] == 1, "single-layer stack"
    b, s, d = x_prompt.shape
    db, nt, _ = x_sample.shape
    assert CHUNK % nt == 0 and (db * nt) % CHUNK == 0
    tail_start = ((s - 1) // CHUNK) * CHUNK

    bf = lambda w: w[0].astype(BF16)
    pre_w = (norm_ffn1, bf(ffn1_w_in), bf(ffn1_w_out), norm_mix, bf(w_in), sgu_ln_g, sgu_ln_b)
    lam_w = (lambda_q1, lambda_k1, lambda_q2, lambda_k2, subln)
    post_w = (bf(w_proj_attn), bf(w_proj_sgu), bf(w_out), norm_ffn2, bf(ffn2_w_in), bf(ffn2_w_out),
              norm_final.reshape(1, d))

    h, k, v, gu, vs, ga, gb, qh, kh, vh = _pre_stage(x_prompt.reshape(b * s, d), *pre_w,
                                                     head_major_seq=s)
    hs, ks, vsm, gus, vss, gas, gbs, qs = _pre_stage(x_sample.reshape(db * nt, d), *pre_w)
    n_pool = cache_k.shape[1]
    a, a_s = _attention(qh, kh, vh, qs.reshape(db, nt, Q_COLS), ks, vsm,
                        cache_k.reshape(n_pool, PAGE_SIZE * N_HEADS, KEY_DIM),
                        cache_v.reshape(n_pool, PAGE_SIZE * N_HEADS, VALUE_DIM), page_table, *lam_w)

    y_prompt = _post_stage(h, a.reshape(b * s, ATT_WIDTH), gu, vs, ga, gb,
                           sgu_w[0], sgu_b[0].T, *post_w, seq_per_chunk=1)
    k_prompt = k.reshape(1, b, s, N_HEADS, KEY_DIM)
    v_prompt = v.reshape(1, b, s, N_HEADS, VALUE_DIM)
    sgu_v_prompt = vs.reshape(b, s, SGU_WIDTH)[None, :, tail_start:]

    reps = CHUNK // nt
    wsp_s = jnp.tile(sgu_w[0][:, :nt, :nt], (1, reps, reps))
    bsp_s = jnp.tile(sgu_b[0][:, :nt], (1, reps)).T
    y_sample = _post_stage(hs, a_s.reshape(db * nt, ATT_WIDTH), gus, vss, gas, gbs,
                           wsp_s, bsp_s, *post_w, seq_per_chunk=reps)
    k_sample = ks.reshape(1, db, nt, N_HEADS, KEY_DIM)
    v_sample = vsm.reshape(1, db, nt, N_HEADS, VALUE_DIM)
    sgu_v_sample = vss.reshape(1, db, nt, SGU_WIDTH)

    return (y_prompt.reshape(b, s, d), y_sample.reshape(db, nt, d), k_prompt, v_prompt,
            sgu_v_prompt, k_sample, v_sample, sgu_v_sample)
```

```python
import functools
import math

import jax
import jax.numpy as jnp
from jax import lax
from jax.experimental import pallas as pl
from jax.experimental.pallas import tpu as pltpu

D_MODEL = 1024
D_FF = 2816
N_HEADS = 4
HEAD_DIM = 64
KEY_DIM = 2 * HEAD_DIM
VALUE_DIM = 2 * HEAD_DIM
Q_COLS = N_HEADS * KEY_DIM
ATT_WIDTH = N_HEADS * VALUE_DIM
SGU_WIDTH = 512
SGU_GROUPS = 4
SGU_GROUP_DIM = SGU_WIDTH // SGU_GROUPS
CHUNK = 128
PAGE_SIZE = 128
SCALE = HEAD_DIM ** -0.5
RMS_EPS = 1e-6
LN_EPS = 1e-5
LAMBDA_INIT = 0.8 - 0.6 * math.exp(-0.3 * 0)
SQRT_HALF = math.sqrt(0.5)
LOG2_E = math.log2(math.e)

_Q0, _K0, _V0, _U0, _VG0, _GA0, _GB0, _END = 0, 512, 1024, 1536, 2048, 2560, 3584, 4608

BF16 = jnp.bfloat16
F32 = jnp.float32

V7X_VMEM_LIMIT_BYTES = 60 * 1024 * 1024

TOKEN_TILE = 256
ATTN_Q_TILE = 512
ATTN_KV_TILE = 256
ATTN_COL_GROUP = 256
ATTN_SUM_ROWS = 16
PAGES_PER_STEP = 16
SAMPLE_PARTS = 2
SAMPLE_ROWS = 128


def _rms(x, g, eps=RMS_EPS):
    return x * lax.rsqrt(jnp.mean(x * x, axis=-1, keepdims=True) + eps) * g


def _gelu(x):
    return 0.5 * x * (1.0 + lax.erf(x * SQRT_HALF))


def _swiglu_half_step(x, g_ref, w_in_ref, w_out_ref):
    xn = _rms(x, g_ref[...]).astype(BF16)
    gate_up = jnp.dot(xn, w_in_ref[...], preferred_element_type=F32)
    gate = gate_up[:, :D_FF]
    up = gate_up[:, D_FF:]
    act = (gate * jax.nn.sigmoid(gate) * up).astype(BF16)
    return x + 0.5 * jnp.dot(act, w_out_ref[...], preferred_element_type=F32)


def _diff_lambda(lq1_ref, lk1_ref, lq2_ref, lk2_ref):
    s1 = jnp.sum(lq1_ref[...] * lk1_ref[...], axis=-1, keepdims=True)
    s2 = jnp.sum(lq2_ref[...] * lk2_ref[...], axis=-1, keepdims=True)
    return jnp.exp(s1) - jnp.exp(s2) + LAMBDA_INIT


def _sub_norm(o1, o2, lam, subln):
    o = o1 - lam * o2
    return _rms(o, subln) * (1.0 - LAMBDA_INIT)


def _pre_kernel(x_ref, n1_ref, w1_ref, w2_ref, nm_ref, win_ref, lng_ref, lnb_ref,
                h_ref, k_ref, v_ref, gu_ref, vs_ref, ga_ref, gb_ref, *attn_refs):
    h = _swiglu_half_step(x_ref[...], n1_ref, w1_ref, w2_ref)
    h_ref[...] = h
    nrm = _rms(h, nm_ref[...]).astype(BF16)

    def proj(lo, hi):
        return jnp.dot(nrm, win_ref[:, lo:hi], preferred_element_type=F32)

    q = proj(_Q0, _K0) * (SCALE * LOG2_E)
    k = proj(_K0, _V0)
    v = proj(_V0, _U0)
    head = lambda z, hd: z[:, hd * KEY_DIM:(hd + 1) * KEY_DIM]
    tm = x_ref.shape[0]
    for hd in range(N_HEADS):
        k_ref[pl.ds(hd, tm, stride=N_HEADS), :] = head(k, hd)
        v_ref[pl.ds(hd, tm, stride=N_HEADS), :] = head(v, hd)
    if len(attn_refs) == 3:
        qt_ref, kh_ref, vt_ref = attn_refs
        for hd in range(N_HEADS):
            qt_ref[hd] = head(q, hd).T.astype(BF16)
            kh_ref[hd] = head(k, hd).astype(BF16)
            vt_ref[hd] = head(v, hd).T.astype(BF16)
    else:
        attn_refs[0][...] = q
    gu_ref[...] = _gelu(proj(_U0, _VG0))
    gv = _gelu(proj(_VG0, _GA0))
    mu = jnp.mean(gv, axis=-1, keepdims=True)
    gc = gv - mu
    vs_ref[...] = (gc * lax.rsqrt(jnp.mean(gc * gc, axis=-1, keepdims=True) + LN_EPS)
                   * lng_ref[...] + lnb_ref[...])
    ga_ref[...] = jax.nn.sigmoid(proj(_GA0, _GB0))
    gb_ref[...] = jax.nn.sigmoid(proj(_GB0, _END))


def _resident(shape):
    return pl.BlockSpec(shape, lambda *_: (0,) * len(shape), pipeline_mode=pl.Buffered(1))


def _rows(tm, width):
    return pl.BlockSpec((tm, width), lambda i: (i, 0))


def _pre_stage(x, n1, w1, w2, nm, win, lng, lnb, *, head_major_seq=None):
    t = x.shape[0]
    tm = TOKEN_TILE
    assert t % tm == 0
    cache_rows = _rows(tm * N_HEADS, KEY_DIM)
    cache_shape = jax.ShapeDtypeStruct((t * N_HEADS, KEY_DIM), F32)
    flat = lambda w: (_rows(tm, w), jax.ShapeDtypeStruct((t, w), F32))
    outs = [flat(D_MODEL), (cache_rows, cache_shape), (cache_rows, cache_shape), flat(SGU_WIDTH),
            flat(SGU_WIDTH), flat(D_MODEL), flat(D_MODEL)]
    if head_major_seq is not None:
        seq = head_major_seq
        assert seq % tm == 0 and t % seq == 0
        per = seq // tm
        rows = (pl.BlockSpec((None, N_HEADS, tm, KEY_DIM), lambda i: (i // per, 0, i % per, 0)),
                jax.ShapeDtypeStruct((t // seq, N_HEADS, seq, KEY_DIM), BF16))
        cols = (pl.BlockSpec((None, N_HEADS, KEY_DIM, tm), lambda i: (i // per, 0, 0, i % per)),
                jax.ShapeDtypeStruct((t // seq, N_HEADS, KEY_DIM, seq), BF16))
        outs += [cols, rows, cols]
    else:
        outs += [flat(Q_COLS)]
    return pl.pallas_call(
        _pre_kernel,
        grid=(t // tm,),
        in_specs=[_rows(tm, D_MODEL), _resident(n1.shape), _resident(w1.shape), _resident(w2.shape),
                  _resident(nm.shape), _resident(win.shape), _resident(lng.shape), _resident(lnb.shape)],
        out_specs=[o[0] for o in outs],
        out_shape=[o[1] for o in outs],
        compiler_params=pltpu.CompilerParams(dimension_semantics=("parallel",),
                                             vmem_limit_bytes=V7X_VMEM_LIMIT_BYTES),
        name="pre_stage",
    )(x, n1, w1, w2, nm, win, lng, lnb)


def _attn_prompt_body(qi, qt_ref, k_ref, vt_ref, lq1_ref, lk1_ref, lq2_ref, lk2_ref, subln_t_ref,
                      o_ref, m_ref, acc_ref, alpha_ref, *bufs, overlap_work=None):
    tq, tk = ATTN_Q_TILE, ATTN_KV_TILE
    assert tq == 2 * tk
    st, pt, bm = bufs[:4], bufs[4:8], bufs[8:]

    qt = qt_ref[...]
    dim = lax.broadcasted_iota(jnp.int32, qt.shape, 0)
    zero = jnp.zeros_like(qt)
    qqt = jnp.concatenate([jnp.where(dim < HEAD_DIM, qt, zero),
                           jnp.where(dim >= HEAD_DIM, qt, zero)], axis=1)

    groups = [slice(c, c + ATTN_COL_GROUP) for c in range(0, 2 * tq, ATTN_COL_GROUP)]

    def scores(blk, half, cols):
        start = pl.multiple_of(blk * tk, 2 * tk)
        s2 = jnp.dot(k_ref[pl.ds(start, 2 * tk), :], qqt[:, cols], preferred_element_type=F32)
        for i, s in enumerate((s2[:tk], s2[tk:])):
            st[2 * half + i][:, cols] = s
            bm[2 * half + i][:, cols] = jnp.max(s, axis=0, keepdims=True)

    ones_rows = jnp.ones((ATTN_SUM_ROWS, tk), BF16)

    def weighted(blk, pt_ref, cols):
        start = pl.multiple_of(blk * tk, tk)
        lhs = jnp.concatenate([vt_ref[:, pl.ds(start, tk)], ones_rows], axis=0)
        return jnp.dot(lhs, pt_ref[:, cols], preferred_element_type=F32)

    def softmax(buf, cols, key_offset=None):
        s = st[buf][:, cols]
        if key_offset is None:
            s_max = bm[buf][:, cols]
        else:
            key = lax.broadcasted_iota(jnp.int32, s.shape, 0) + key_offset
            qpos = lax.broadcasted_iota(jnp.int32, s.shape, 1) + cols.start % tq
            s = jnp.where(key <= qpos, s, -jnp.inf)
            s_max = jnp.max(s, axis=0, keepdims=True)
        m_old = m_ref[:, cols]
        m_new = jnp.maximum(m_old, s_max)
        alpha = jnp.exp2(m_old - m_new)
        p = jnp.exp2(s - m_new)
        m_ref[:, cols] = m_new
        pt[buf][:, cols] = p.astype(BF16)
        return alpha

    def fold_pending(t, half, cols, alpha_a, alpha_b):
        pv_a = weighted(t - 2, pt[2 * half], cols)
        pv_b = weighted(t - 1, pt[2 * half + 1], cols)
        acc = acc_ref[:, cols] + alpha_ref[:, cols] * pv_a + pv_b
        return (alpha_a if alpha_b is None else alpha_a * alpha_b) * acc

    def pair(t, parity, first=False):
        cur, nxt = parity, 1 - parity
        for cols in groups:
            scores(t + 2, nxt, cols)
            alpha_a = softmax(2 * cur, cols)
            alpha_b = softmax(2 * cur + 1, cols)
            if not first:
                acc_ref[:, cols] = fold_pending(t, nxt, cols, alpha_a, alpha_b)
            alpha_ref[:, cols] = alpha_b

    def diagonal(t, parity, first=False):
        cur, nxt = parity, 1 - parity
        for cols in groups:
            first_q = cols.start % tq
            sees_all_lo = first_q >= tk - 1
            sees_no_hi = first_q + ATTN_COL_GROUP <= tk
            alpha_a = softmax(2 * cur, cols, key_offset=None if sees_all_lo else 0)
            alpha_b = None if sees_no_hi else softmax(2 * cur + 1, cols, key_offset=tk)
            pv = weighted(t, pt[2 * cur], cols)
            if alpha_b is not None:
                pv = alpha_b * pv + weighted(t + 1, pt[2 * cur + 1], cols)
            if not first:
                pv = pv + fold_pending(t, nxt, cols, alpha_a, alpha_b)
            acc_ref[:, cols] = pv

    m_ref[...] = jnp.full(m_ref.shape, -jnp.inf, F32)
    acc_ref[...] = jnp.zeros(acc_ref.shape, F32)
    for cols in groups:
        scores(0, 0, cols)
    if overlap_work is not None:
        overlap_work()

    @pl.when(qi == 0)
    def _():
        diagonal(0, 0, first=True)

    @pl.when(qi > 0)
    def _():
        pair(0, 0, first=True)

    def body(u, carry):
        for parity in (0, 1):
            @pl.when(u % 2 == parity)
            def _():
                pair(2 * u, parity)
        return carry

    lax.fori_loop(1, qi, body, 0)
    for parity in (0, 1):
        @pl.when((qi > 0) & (qi % 2 == parity))
        def _():
            diagonal(2 * qi, parity)

    ot = acc_ref[:VALUE_DIM, :] / acc_ref[VALUE_DIM:VALUE_DIM + 1, :]
    lam = _diff_lambda(lq1_ref, lk1_ref, lq2_ref, lk2_ref)
    o = ot[:, :tq] - lam * ot[:, tq:]
    y = (o * lax.rsqrt(jnp.mean(o * o, axis=0, keepdims=True) + RMS_EPS) * subln_t_ref[...]
         * (1.0 - LAMBDA_INIT))
    o_ref[...] = y.T.astype(BF16)


def _attn_sample_phases(c, n_chunks, q_ref, kn_ref, vn_ref, lq1_ref, lk1_ref, lq2_ref, lk2_ref,
                        subln_ref, k_pages, v_pages, o_ref, m_ref, l_ref, acc_ref, kb_ref, vb_ref):
    npg = len(k_pages)
    nt = q_ref.shape[0]
    nrow = SAMPLE_ROWS

    q = q_ref[...]
    qrows = jnp.concatenate([q] * (nrow // nt), axis=0)
    dim = lax.broadcasted_iota(jnp.int32, qrows.shape, 1)
    row = lax.broadcasted_iota(jnp.int32, qrows.shape, 0)
    live = (dim // HEAD_DIM == row // nt) & (row < 2 * nt * N_HEADS)
    qm_f32 = jnp.where(live, qrows, 0.0)
    qm = qm_f32.astype(BF16)
    contract_last = (((1,), (1,)), ((), ()))

    def start():
        @pl.when(c == 0)
        def _():
            head_rows = lambda ref, hd: ref[pl.ds(hd, nt, stride=N_HEADS), :]
            kn = jnp.concatenate([head_rows(kn_ref, hd) for hd in range(N_HEADS)], axis=1)
            vn = jnp.concatenate([head_rows(vn_ref, hd) for hd in range(N_HEADS)], axis=1)
            s = lax.dot_general(qm_f32, kn, contract_last, preferred_element_type=F32)
            qry_t = lax.broadcasted_iota(jnp.int32, s.shape, 0) % nt
            key_t = lax.broadcasted_iota(jnp.int32, s.shape, 1)
            s = jnp.where(key_t <= qry_t, s, -jnp.inf)
            m = jnp.max(s, axis=1, keepdims=True)
            p = jnp.exp2(s - m)
            m_ref[...] = m
            l_ref[...] = jnp.sum(p, axis=1, keepdims=True)
            acc_ref[...] = jnp.dot(p, vn, preferred_element_type=F32)

    pages_per_part = npg // SAMPLE_PARTS
    part_rows = [slice(i * pages_per_part * PAGE_SIZE, (i + 1) * pages_per_part * PAGE_SIZE)
                 for i in range(SAMPLE_PARTS)]

    def relayout(pages, buf_ref, part):
        for i in range(part * pages_per_part, (part + 1) * pages_per_part):
            rows = slice(i * PAGE_SIZE, (i + 1) * PAGE_SIZE)
            for hd in range(N_HEADS):
                lanes = slice(hd * KEY_DIM, (hd + 1) * KEY_DIM)
                tokens = pl.ds(hd, PAGE_SIZE, stride=N_HEADS)
                buf_ref[rows, lanes] = pages[i][tokens, :].astype(BF16)

    def accumulate():
        scores = []
        for part, rows in enumerate(part_rows):
            relayout(k_pages, kb_ref, part)
            scores.append(lax.dot_general(qm, kb_ref[rows, :], contract_last,
                                          preferred_element_type=F32))
        m_old = m_ref[...]
        m_new = m_old
        for s in scores:
            m_new = jnp.maximum(m_new, jnp.max(s, axis=1, keepdims=True))
        alpha = jnp.exp2(m_old - m_new)
        m_ref[...] = m_new
        l = alpha * l_ref[...]
        acc = alpha * acc_ref[...]
        for part, rows in enumerate(part_rows):
            relayout(v_pages, vb_ref, part)
            p = jnp.exp2(scores[part] - m_new)
            l = l + jnp.sum(p, axis=1, keepdims=True)
            acc = acc + jnp.dot(p.astype(BF16), vb_ref[rows, :], preferred_element_type=F32)
        l_ref[...] = l
        acc_ref[...] = acc

    def finish():
        @pl.when(c == n_chunks - 1)
        def _():
            o = acc_ref[...] / l_ref[...]
            lam = _diff_lambda(lq1_ref, lk1_ref, lq2_ref, lk2_ref)
            for h in range(N_HEADS):
                r0 = h * 2 * nt
                o1 = o[r0:r0 + nt, h * VALUE_DIM:(h + 1) * VALUE_DIM]
                o2 = o[r0 + nt:r0 + 2 * nt, h * VALUE_DIM:(h + 1) * VALUE_DIM]
                o_ref[:, h * VALUE_DIM:(h + 1) * VALUE_DIM] = _sub_norm(
                    o1, o2, lam, subln_ref[...])

    return start, accumulate, finish


N_PROMPT_SCRATCH = 3 + 12


def _attn_kernel(pt_ref, qt_ref, k_ref, vt_ref, lq1_ref, lk1_ref, lq2_ref, lk2_ref, subln_t_ref,
                 qs_ref, kn_ref, vn_ref, subln_ref, *rest, n_chunks):
    del pt_ref
    npg = PAGES_PER_STEP
    k_pages, v_pages = rest[:npg], rest[npg:2 * npg]
    o_ref, os_ref = rest[2 * npg:2 * npg + 2]
    scratch = rest[2 * npg + 2:]
    qi = pl.program_id(2)
    lam_refs = (lq1_ref, lk1_ref, lq2_ref, lk2_ref)
    start, accumulate, finish = _attn_sample_phases(
        qi % n_chunks, n_chunks, qs_ref, kn_ref, vn_ref, *lam_refs, subln_ref,
        k_pages, v_pages, os_ref, *scratch[N_PROMPT_SCRATCH:])
    start()
    _attn_prompt_body(qi, qt_ref, k_ref, vt_ref, *lam_refs, subln_t_ref, o_ref,
                      *scratch[:N_PROMPT_SCRATCH], overlap_work=accumulate)
    finish()


def _attention(qt, k, vt, qs, k_new, v_new, cache_k, cache_v, page_table, lq1, lk1, lq2, lk2,
               subln):
    b, _, s, _ = k.shape
    db, nt, _ = qs.shape
    n_pages = page_table.shape[1]
    tq, tk, npg = ATTN_Q_TILE, ATTN_KV_TILE, PAGES_PER_STEP
    assert s % tq == 0 and n_pages % npg == 0 and npg % SAMPLE_PARTS == 0
    assert SAMPLE_ROWS % nt == 0 and 2 * nt * N_HEADS <= SAMPLE_ROWS
    nq, n_chunks = s // tq, n_pages // npg
    assert nq % n_chunks == 0 and b * N_HEADS * (nq // n_chunks) == db
    seqs_per_head = nq // n_chunks

    def seq(bi, hi, qi):
        return (bi * N_HEADS + hi) * seqs_per_head + qi // n_chunks

    small = lambda a: pl.BlockSpec(a.shape, lambda bi, hi, qi, pt: (0, 0))
    new_rows = pl.BlockSpec((nt * N_HEADS, KEY_DIM), lambda bi, hi, qi, pt: (seq(bi, hi, qi), 0))
    per_seq = pl.BlockSpec((None, nt, Q_COLS), lambda bi, hi, qi, pt: (seq(bi, hi, qi), 0, 0))

    def page(i):
        return pl.BlockSpec(
            (None, PAGE_SIZE * N_HEADS, KEY_DIM),
            lambda bi, hi, qi, pt: (pt[seq(bi, hi, qi), (qi % n_chunks) * npg + i], 0, 0))

    subln_t = subln.reshape(VALUE_DIM, 1)
    grid_spec = pltpu.PrefetchScalarGridSpec(
        num_scalar_prefetch=1,
        grid=(b, N_HEADS, nq),
        in_specs=([pl.BlockSpec((None, None, KEY_DIM, tq), lambda bi, hi, qi, pt: (bi, hi, 0, qi)),
                   pl.BlockSpec((None, None, s, KEY_DIM), lambda bi, hi, qi, pt: (bi, hi, 0, 0)),
                   pl.BlockSpec((None, None, VALUE_DIM, s), lambda bi, hi, qi, pt: (bi, hi, 0, 0)),
                   small(lq1), small(lk1), small(lq2), small(lk2), small(subln_t),
                   per_seq, new_rows, new_rows, small(subln)]
                  + [page(i) for i in range(npg)] * 2),
        out_specs=[pl.BlockSpec((None, tq, VALUE_DIM), lambda bi, hi, qi, pt: (bi, qi, hi)),
                   per_seq],
        scratch_shapes=(
            [pltpu.VMEM((1, 2 * tq), F32),
             pltpu.VMEM((VALUE_DIM + ATTN_SUM_ROWS, 2 * tq), F32),
             pltpu.VMEM((1, 2 * tq), F32)]
            + [pltpu.VMEM((tk, 2 * tq), F32)] * 4 + [pltpu.VMEM((tk, 2 * tq), BF16)] * 4
            + [pltpu.VMEM((1, 2 * tq), F32)] * 4
            + [pltpu.VMEM((SAMPLE_ROWS, 1), F32), pltpu.VMEM((SAMPLE_ROWS, 1), F32),
               pltpu.VMEM((SAMPLE_ROWS, ATT_WIDTH), F32),
               pltpu.VMEM((npg * PAGE_SIZE, Q_COLS), BF16),
               pltpu.VMEM((npg * PAGE_SIZE, ATT_WIDTH), BF16)]),
    )
    return pl.pallas_call(
        functools.partial(_attn_kernel, n_chunks=n_chunks),
        grid_spec=grid_spec,
        out_shape=[jax.ShapeDtypeStruct((b, s, ATT_WIDTH), BF16),
                   jax.ShapeDtypeStruct((db, nt, ATT_WIDTH), F32)],
        compiler_params=pltpu.CompilerParams(
            dimension_semantics=("parallel", "parallel", "arbitrary"),
            vmem_limit_bytes=V7X_VMEM_LIMIT_BYTES),
        name="attention",
    )(page_table, qt, k, vt, lq1, lk1, lq2, lk2, subln_t, qs, k_new, v_new, subln,
      *([cache_k] * npg), *([cache_v] * npg))


def _post_kernel(h_ref, a_ref, gu_ref, vs_ref, ga_ref, gb_ref, wsp_ref, bsp_ref, wpa_ref, wpb_ref,
                 wo_ref, n2_ref, w1_ref, w2_ref, nf_ref, y_ref, *, seq_per_chunk):
    tm = h_ref.shape[0]
    row = lax.broadcasted_iota(jnp.int32, (CHUNK, CHUNK), 0)
    col = lax.broadcasted_iota(jnp.int32, (CHUNK, CHUNK), 1)
    seq_len = CHUNK // seq_per_chunk
    keep = (col <= row) & (row // seq_len == col // seq_len)
    bias = bsp_ref[...]
    n_chunks = tm // CHUNK
    mixed = [[None] * SGU_GROUPS for _ in range(n_chunks)]
    for g in range(SGU_GROUPS):
        lanes = slice(g * SGU_GROUP_DIM, (g + 1) * SGU_GROUP_DIM)
        w = jnp.where(keep, wsp_ref[g], 0.0).astype(BF16)
        vg = jnp.concatenate([vs_ref[ci * CHUNK:(ci + 1) * CHUNK, lanes].astype(BF16)
                              for ci in range(n_chunks)], axis=1)
        out = jnp.dot(w, vg, preferred_element_type=F32) + bias[:, g:g + 1]
        for ci in range(n_chunks):
            mixed[ci][g] = out[:, ci * SGU_GROUP_DIM:(ci + 1) * SGU_GROUP_DIM]
    sp = jnp.concatenate([jnp.concatenate(row, axis=1) for row in mixed], axis=0)
    sgu = (gu_ref[...] * sp).astype(BF16)
    m = (ga_ref[...] * jnp.dot(a_ref[...].astype(BF16), wpa_ref[...], preferred_element_type=F32)
         + gb_ref[...] * jnp.dot(sgu, wpb_ref[...], preferred_element_type=F32))
    h = h_ref[...] + jnp.dot(m.astype(BF16), wo_ref[...], preferred_element_type=F32)
    out = _swiglu_half_step(h, n2_ref, w1_ref, w2_ref)
    y_ref[...] = _rms(out, nf_ref[...])


def _post_stage(h, a, gu, vs, ga, gb, wsp, bsp, wpa, wpb, wo, n2, w1, w2, nf, *, seq_per_chunk):
    t = h.shape[0]
    tm = TOKEN_TILE
    assert t % tm == 0 and tm % CHUNK == 0
    widths = (D_MODEL, ATT_WIDTH, SGU_WIDTH, SGU_WIDTH, D_MODEL, D_MODEL)
    consts = (wsp, bsp, wpa, wpb, wo, n2, w1, w2, nf)
    return pl.pallas_call(
        functools.partial(_post_kernel, seq_per_chunk=seq_per_chunk),
        grid=(t // tm,),
        in_specs=[_rows(tm, w) for w in widths] + [_resident(c.shape) for c in consts],
        out_specs=_rows(tm, D_MODEL),
        out_shape=jax.ShapeDtypeStruct((t, D_MODEL), F32),
        compiler_params=pltpu.CompilerParams(dimension_semantics=("parallel",),
                                             vmem_limit_bytes=V7X_VMEM_LIMIT_BYTES),
        name="post_stage",
    )(h, a, gu, vs, ga, gb, *consts)


def kernel(x_prompt, x_sample, cache_k, cache_v, page_table, norm_ffn1, ffn1_w_in, ffn1_w_out,
           norm_mix, w_in, lambda_q1, lambda_k1, lambda_q2, lambda_k2, subln, sgu_ln_g, sgu_ln_b,
           sgu_w, sgu_b, w_proj_attn, w_proj_sgu, w_out, norm_ffn2, ffn2_w_in, ffn2_w_out,
           norm_final):
    assert norm_ffn1.shape[0] == 1, "single-layer stack"
    b, s, d = x_prompt.shape
    db, nt, _ = x_sample.shape
    assert CHUNK % nt == 0 and (db * nt) % CHUNK == 0
    tail_start = ((s - 1) // CHUNK) * CHUNK

    bf = lambda w: w[0].astype(BF16)
    pre_w = (norm_ffn1, bf(ffn1_w_in), bf(ffn1_w_out), norm_mix, bf(w_in), sgu_ln_g, sgu_ln_b)
    lam_w = (lambda_q1, lambda_k1, lambda_q2, lambda_k2, subln)
    post_w = (bf(w_proj_attn), bf(w_proj_sgu), bf(w_out), norm_ffn2, bf(ffn2_w_in), bf(ffn2_w_out),
              norm_final.reshape(1, d))

    h, k, v, gu, vs, ga, gb, qh, kh, vh = _pre_stage(x_prompt.reshape(b * s, d), *pre_w,
                                                     head_major_seq=s)
    hs, ks, vsm, gus, vss, gas, gbs, qs = _pre_stage(x_sample.reshape(db * nt, d), *pre_w)
    n_pool = cache_k.shape[1]
    a, a_s = _attention(qh, kh, vh, qs.reshape(db, nt, Q_COLS), ks, vsm,
                        cache_k.reshape(n_pool, PAGE_SIZE * N_HEADS, KEY_DIM),
                        cache_v.reshape(n_pool, PAGE_SIZE * N_HEADS, VALUE_DIM), page_table, *lam_w)

    y_prompt = _post_stage(h, a.reshape(b * s, ATT_WIDTH), gu, vs, ga, gb,
                           sgu_w[0], sgu_b[0].T, *post_w, seq_per_chunk=1)
    k_prompt = k.reshape(1, b, s, N_HEADS, KEY_DIM)
    v_prompt = v.reshape(1, b, s, N_HEADS, VALUE_DIM)
    sgu_v_prompt = vs.reshape(b, s, SGU_WIDTH)[None, :, tail_start:]

    reps = CHUNK // nt
    wsp_s = jnp.tile(sgu_w[0][:, :nt, :nt], (1, reps, reps))
    bsp_s = jnp.tile(sgu_b[0][:, :nt], (1, reps)).T
    y_sample = _post_stage(hs, a_s.reshape(db * nt, ATT_WIDTH), gus, vss, gas, gbs,
                           wsp_s, bsp_s, *post_w, seq_per_chunk=reps)
    k_sample = ks.reshape(1, db, nt, N_HEADS, KEY_DIM)
    v_sample = vsm.reshape(1, db, nt, N_HEADS, VALUE_DIM)
    sgu_v_sample = vss.reshape(1, db, nt, SGU_WIDTH)

    return (y_prompt.reshape(b, s, d), y_sample.reshape(db, nt, d), k_prompt, v_prompt,
            sgu_v_prompt, k_sample, v_sample, sgu_v_sample)
```

```python
import functools
import math

import jax
import jax.numpy as jnp
from jax import lax
from jax.experimental import pallas as pl
from jax.experimental.pallas import tpu as pltpu

D_MODEL = 1024
D_FF = 2816
N_HEADS = 4
HEAD_DIM = 64
KEY_DIM = 2 * HEAD_DIM
VALUE_DIM = 2 * HEAD_DIM
Q_COLS = N_HEADS * KEY_DIM
ATT_WIDTH = N_HEADS * VALUE_DIM
SGU_WIDTH = 512
SGU_GROUPS = 4
SGU_GROUP_DIM = SGU_WIDTH // SGU_GROUPS
CHUNK = 128
PAGE_SIZE = 128
SCALE = HEAD_DIM ** -0.5
RMS_EPS = 1e-6
LN_EPS = 1e-5
LAMBDA_INIT = 0.8 - 0.6 * math.exp(-0.3 * 0)
SQRT_HALF = math.sqrt(0.5)
LOG2_E = math.log2(math.e)

_Q0, _K0, _V0, _U0, _VG0, _GA0, _GB0, _END = 0, 512, 1024, 1536, 2048, 2560, 3584, 4608

BF16 = jnp.bfloat16
F32 = jnp.float32

V7X_VMEM_LIMIT_BYTES = 60 * 1024 * 1024

TOKEN_TILE = 256
ATTN_Q_TILE = 512
ATTN_KV_TILE = 256
ATTN_COL_GROUP = 256
ATTN_SUM_ROWS = 16
PAGES_PER_STEP = 16
SAMPLE_PARTS = 2
SAMPLE_ROWS = 128


def _rms(x, g, eps=RMS_EPS):
    return x * lax.rsqrt(jnp.mean(x * x, axis=-1, keepdims=True) + eps) * g


def _gelu(x):
    return 0.5 * x * (1.0 + lax.erf(x * SQRT_HALF))


def _swiglu_half_step(x, g_ref, w_in_ref, w_out_ref):
    xn = _rms(x, g_ref[...]).astype(BF16)
    gate_up = jnp.dot(xn, w_in_ref[...], preferred_element_type=F32)
    gate = gate_up[:, :D_FF]
    up = gate_up[:, D_FF:]
    act = (gate * jax.nn.sigmoid(gate) * up).astype(BF16)
    return x + 0.5 * jnp.dot(act, w_out_ref[...], preferred_element_type=F32)


def _diff_lambda(lq1_ref, lk1_ref, lq2_ref, lk2_ref):
    s1 = jnp.sum(lq1_ref[...] * lk1_ref[...], axis=-1, keepdims=True)
    s2 = jnp.sum(lq2_ref[...] * lk2_ref[...], axis=-1, keepdims=True)
    return jnp.exp(s1) - jnp.exp(s2) + LAMBDA_INIT


def _sub_norm(o1, o2, lam, subln):
    o = o1 - lam * o2
    return _rms(o, subln) * (1.0 - LAMBDA_INIT)


def _pre_kernel(x_ref, n1_ref, w1_ref, w2_ref, nm_ref, win_ref, lng_ref, lnb_ref,
                h_ref, k_ref, v_ref, gu_ref, vs_ref, ga_ref, gb_ref, *attn_refs):
    h = _swiglu_half_step(x_ref[...], n1_ref, w1_ref, w2_ref)
    h_ref[...] = h
    nrm = _rms(h, nm_ref[...]).astype(BF16)

    def proj(lo, hi):
        return jnp.dot(nrm, win_ref[:, lo:hi], preferred_element_type=F32)

    q = proj(_Q0, _K0) * (SCALE * LOG2_E)
    k = proj(_K0, _V0)
    v = proj(_V0, _U0)
    head = lambda z, hd: z[:, hd * KEY_DIM:(hd + 1) * KEY_DIM]
    tm = x_ref.shape[0]
    for hd in range(N_HEADS):
        k_ref[pl.ds(hd, tm, stride=N_HEADS), :] = head(k, hd)
        v_ref[pl.ds(hd, tm, stride=N_HEADS), :] = head(v, hd)
    if len(attn_refs) == 3:
        qt_ref, kh_ref, vt_ref = attn_refs
        for hd in range(N_HEADS):
            qt_ref[hd] = head(q, hd).T.astype(BF16)
            kh_ref[hd] = head(k, hd).astype(BF16)
            vt_ref[hd] = head(v, hd).T.astype(BF16)
    else:
        attn_refs[0][...] = q
    gu_ref[...] = _gelu(proj(_U0, _VG0))
    gv = _gelu(proj(_VG0, _GA0))
    mu = jnp.mean(gv, axis=-1, keepdims=True)
    gc = gv - mu
    vs_ref[...] = (gc * lax.rsqrt(jnp.mean(gc * gc, axis=-1, keepdims=True) + LN_EPS)
                   * lng_ref[...] + lnb_ref[...])
    ga_ref[...] = jax.nn.sigmoid(proj(_GA0, _GB0))
    gb_ref[...] = jax.nn.sigmoid(proj(_GB0, _END))


def _resident(shape):
    return pl.BlockSpec(shape, lambda *_: (0,) * len(shape), pipeline_mode=pl.Buffered(1))


def _rows(tm, width):
    return pl.BlockSpec((tm, width), lambda i: (i, 0))


def _pre_stage(x, n1, w1, w2, nm, win, lng, lnb, *, head_major_seq=None):
    t = x.shape[0]
    tm = TOKEN_TILE
    assert t % tm == 0
    cache_rows = _rows(tm * N_HEADS, KEY_DIM)
    cache_shape = jax.ShapeDtypeStruct((t * N_HEADS, KEY_DIM), F32)
    flat = lambda w: (_rows(tm, w), jax.ShapeDtypeStruct((t, w), F32))
    outs = [flat(D_MODEL), (cache_rows, cache_shape), (cache_rows, cache_shape), flat(SGU_WIDTH),
            flat(SGU_WIDTH), flat(D_MODEL), flat(D_MODEL)]
    if head_major_seq is not None:
        seq = head_major_seq
        assert seq % tm == 0 and t % seq == 0
        per = seq // tm
        rows = (pl.BlockSpec((None, N_HEADS, tm, KEY_DIM), lambda i: (i // per, 0, i % per, 0)),
                jax.ShapeDtypeStruct((t // seq, N_HEADS, seq, KEY_DIM), BF16))
        cols = (pl.BlockSpec((None, N_HEADS, KEY_DIM, tm), lambda i: (i // per, 0, 0, i % per)),
                jax.ShapeDtypeStruct((t // seq, N_HEADS, KEY_DIM, seq), BF16))
        outs += [cols, rows, cols]
    else:
        outs += [flat(Q_COLS)]
    return pl.pallas_call(
        _pre_kernel,
        grid=(t // tm,),
        in_specs=[_rows(tm, D_MODEL), _resident(n1.shape), _resident(w1.shape), _resident(w2.shape),
                  _resident(nm.shape), _resident(win.shape), _resident(lng.shape), _resident(lnb.shape)],
        out_specs=[o[0] for o in outs],
        out_shape=[o[1] for o in outs],
        compiler_params=pltpu.CompilerParams(dimension_semantics=("parallel",),
                                             vmem_limit_bytes=V7X_VMEM_LIMIT_BYTES),
        name="pre_stage",
    )(x, n1, w1, w2, nm, win, lng, lnb)


def _attn_prompt_body(qi, qt_ref, k_ref, vt_ref, lq1_ref, lk1_ref, lq2_ref, lk2_ref, subln_t_ref,
                      o_ref, m_ref, acc_ref, alpha_ref, *bufs, overlap_work=None):
    tq, tk = ATTN_Q_TILE, ATTN_KV_TILE
    assert tq == 2 * tk
    st, pt, bm = bufs[:4], bufs[4:8], bufs[8:]

    qt = qt_ref[...]
    dim = lax.broadcasted_iota(jnp.int32, qt.shape, 0)
    zero = jnp.zeros_like(qt)
    qqt = jnp.concatenate([jnp.where(dim < HEAD_DIM, qt, zero),
                           jnp.where(dim >= HEAD_DIM, qt, zero)], axis=1)

    groups = [slice(c, c + ATTN_COL_GROUP) for c in range(0, 2 * tq, ATTN_COL_GROUP)]

    def scores(blk, half, cols):
        start = pl.multiple_of(blk * tk, 2 * tk)
        s2 = jnp.dot(k_ref[pl.ds(start, 2 * tk), :], qqt[:, cols], preferred_element_type=F32)
        for i, s in enumerate((s2[:tk], s2[tk:])):
            st[2 * half + i][:, cols] = s
            bm[2 * half + i][:, cols] = jnp.max(s, axis=0, keepdims=True)

    ones_rows = jnp.ones((ATTN_SUM_ROWS, tk), BF16)

    def weighted(blk, pt_ref, cols):
        start = pl.multiple_of(blk * tk, tk)
        lhs = jnp.concatenate([vt_ref[:, pl.ds(start, tk)], ones_rows], axis=0)
        return jnp.dot(lhs, pt_ref[:, cols], preferred_element_type=F32)

    def softmax(buf, cols, key_offset=None):
        s = st[buf][:, cols]
        if key_offset is None:
            s_max = bm[buf][:, cols]
        else:
            key = lax.broadcasted_iota(jnp.int32, s.shape, 0) + key_offset
            qpos = lax.broadcasted_iota(jnp.int32, s.shape, 1) + cols.start % tq
            s = jnp.where(key <= qpos, s, -jnp.inf)
            s_max = jnp.max(s, axis=0, keepdims=True)
        m_old = m_ref[:, cols]
        m_new = jnp.maximum(m_old, s_max)
        alpha = jnp.exp2(m_old - m_new)
        p = jnp.exp2(s - m_new)
        m_ref[:, cols] = m_new
        pt[buf][:, cols] = p.astype(BF16)
        return alpha

    def fold_pending(t, half, cols, alpha_a, alpha_b):
        pv_a = weighted(t - 2, pt[2 * half], cols)
        pv_b = weighted(t - 1, pt[2 * half + 1], cols)
        acc = acc_ref[:, cols] + alpha_ref[:, cols] * pv_a + pv_b
        return (alpha_a if alpha_b is None else alpha_a * alpha_b) * acc

    def pair(t, parity, first=False):
        cur, nxt = parity, 1 - parity
        for cols in groups:
            scores(t + 2, nxt, cols)
            alpha_a = softmax(2 * cur, cols)
            alpha_b = softmax(2 * cur + 1, cols)
            if not first:
                acc_ref[:, cols] = fold_pending(t, nxt, cols, alpha_a, alpha_b)
            alpha_ref[:, cols] = alpha_b

    def diagonal(t, parity, first=False):
        cur, nxt = parity, 1 - parity
        for cols in groups:
            first_q = cols.start % tq
            sees_all_lo = first_q >= tk - 1
            sees_no_hi = first_q + ATTN_COL_GROUP <= tk
            alpha_a = softmax(2 * cur, cols, key_offset=None if sees_all_lo else 0)
            alpha_b = None if sees_no_hi else softmax(2 * cur + 1, cols, key_offset=tk)
            pv = weighted(t, pt[2 * cur], cols)
            if alpha_b is not None:
                pv = alpha_b * pv + weighted(t + 1, pt[2 * cur + 1], cols)
            if not first:
                pv = pv + fold_pending(t, nxt, cols, alpha_a, alpha_b)
            acc_ref[:, cols] = pv

    m_ref[...] = jnp.full(m_ref.shape, -jnp.inf, F32)
    acc_ref[...] = jnp.zeros(acc_ref.shape, F32)
    for cols in groups:
        scores(0, 0, cols)
    if overlap_work is not None:
        overlap_work()

    @pl.when(qi == 0)
    def _():
        diagonal(0, 0, first=True)

    @pl.when(qi > 0)
    def _():
        pair(0, 0, first=True)

    def two_pairs(j, carry):
        u = 1 + 2 * j
        pair(2 * u, 1)
        pair(2 * u + 2, 0)
        return carry

    lax.fori_loop(0, (qi - 1) // 2, two_pairs, 0)

    @pl.when((qi > 0) & ((qi - 1) % 2 == 1))
    def _():
        pair(2 * (qi - 1), 1)

    for parity in (0, 1):
        @pl.when((qi > 0) & (qi % 2 == parity))
        def _():
            diagonal(2 * qi, parity)

    ot = acc_ref[:VALUE_DIM, :] * (1.0 / acc_ref[VALUE_DIM:VALUE_DIM + 1, :])
    lam = _diff_lambda(lq1_ref, lk1_ref, lq2_ref, lk2_ref)
    o = ot[:, :tq] - lam * ot[:, tq:]
    y = (o * lax.rsqrt(jnp.mean(o * o, axis=0, keepdims=True) + RMS_EPS) * subln_t_ref[...]
         * (1.0 - LAMBDA_INIT))
    o_ref[...] = y.T.astype(BF16)


def _attn_sample_phases(c, n_chunks, q_ref, kn_ref, vn_ref, lq1_ref, lk1_ref, lq2_ref, lk2_ref,
                        subln_ref, k_pages, v_pages, o_ref, m_ref, l_ref, acc_ref, kb_ref, vb_ref):
    npg = len(k_pages)
    nt = q_ref.shape[0]
    nrow = SAMPLE_ROWS

    q = q_ref[...]
    qrows = jnp.concatenate([q] * (nrow // nt), axis=0)
    dim = lax.broadcasted_iota(jnp.int32, qrows.shape, 1)
    row = lax.broadcasted_iota(jnp.int32, qrows.shape, 0)
    live = (dim // HEAD_DIM == row // nt) & (row < 2 * nt * N_HEADS)
    qm_f32 = jnp.where(live, qrows, 0.0)
    qm = qm_f32.astype(BF16)
    contract_last = (((1,), (1,)), ((), ()))

    def start():
        @pl.when(c == 0)
        def _():
            head_rows = lambda ref, hd: ref[pl.ds(hd, nt, stride=N_HEADS), :]
            kn = jnp.concatenate([head_rows(kn_ref, hd) for hd in range(N_HEADS)], axis=1)
            vn = jnp.concatenate([head_rows(vn_ref, hd) for hd in range(N_HEADS)], axis=1)
            s = lax.dot_general(qm_f32, kn, contract_last, preferred_element_type=F32)
            qry_t = lax.broadcasted_iota(jnp.int32, s.shape, 0) % nt
            key_t = lax.broadcasted_iota(jnp.int32, s.shape, 1)
            s = jnp.where(key_t <= qry_t, s, -jnp.inf)
            m = jnp.max(s, axis=1, keepdims=True)
            p = jnp.exp2(s - m)
            m_ref[...] = m
            l_ref[...] = jnp.sum(p, axis=1, keepdims=True)
            acc_ref[...] = jnp.dot(p, vn, preferred_element_type=F32)

    pages_per_part = npg // SAMPLE_PARTS
    part_rows = [slice(i * pages_per_part * PAGE_SIZE, (i + 1) * pages_per_part * PAGE_SIZE)
                 for i in range(SAMPLE_PARTS)]

    def relayout(pages, buf_ref, part):
        for i in range(part * pages_per_part, (part + 1) * pages_per_part):
            rows = slice(i * PAGE_SIZE, (i + 1) * PAGE_SIZE)
            for hd in range(N_HEADS):
                lanes = slice(hd * KEY_DIM, (hd + 1) * KEY_DIM)
                tokens = pl.ds(hd, PAGE_SIZE, stride=N_HEADS)
                buf_ref[rows, lanes] = pages[i][tokens, :].astype(BF16)

    def accumulate():
        scores = []
        for part, rows in enumerate(part_rows):
            relayout(k_pages, kb_ref, part)
            scores.append(lax.dot_general(qm, kb_ref[rows, :], contract_last,
                                          preferred_element_type=F32))
        m_old = m_ref[...]
        m_new = m_old
        for s in scores:
            m_new = jnp.maximum(m_new, jnp.max(s, axis=1, keepdims=True))
        alpha = jnp.exp2(m_old - m_new)
        m_ref[...] = m_new
        l = alpha * l_ref[...]
        acc = alpha * acc_ref[...]
        for part, rows in enumerate(part_rows):
            relayout(v_pages, vb_ref, part)
            p = jnp.exp2(scores[part] - m_new)
            l = l + jnp.sum(p, axis=1, keepdims=True)
            acc = acc + jnp.dot(p.astype(BF16), vb_ref[rows, :], preferred_element_type=F32)
        l_ref[...] = l
        acc_ref[...] = acc

    def finish():
        @pl.when(c == n_chunks - 1)
        def _():
            o = acc_ref[...] * (1.0 / l_ref[...])
            lam = _diff_lambda(lq1_ref, lk1_ref, lq2_ref, lk2_ref)
            for h in range(N_HEADS):
                r0 = h * 2 * nt
                o1 = o[r0:r0 + nt, h * VALUE_DIM:(h + 1) * VALUE_DIM]
                o2 = o[r0 + nt:r0 + 2 * nt, h * VALUE_DIM:(h + 1) * VALUE_DIM]
                o_ref[:, h * VALUE_DIM:(h + 1) * VALUE_DIM] = _sub_norm(
                    o1, o2, lam, subln_ref[...])

    return start, accumulate, finish


N_PROMPT_SCRATCH = 3 + 12


def _attn_kernel(pt_ref, qt_ref, k_ref, vt_ref, lq1_ref, lk1_ref, lq2_ref, lk2_ref, subln_t_ref,
                 qs_ref, kn_ref, vn_ref, subln_ref, *rest, n_chunks):
    del pt_ref
    npg = PAGES_PER_STEP
    k_pages, v_pages = rest[:npg], rest[npg:2 * npg]
    o_ref, os_ref = rest[2 * npg:2 * npg + 2]
    scratch = rest[2 * npg + 2:]
    qi = pl.program_id(2)
    lam_refs = (lq1_ref, lk1_ref, lq2_ref, lk2_ref)
    start, accumulate, finish = _attn_sample_phases(
        qi % n_chunks, n_chunks, qs_ref, kn_ref, vn_ref, *lam_refs, subln_ref,
        k_pages, v_pages, os_ref, *scratch[N_PROMPT_SCRATCH:])
    start()
    _attn_prompt_body(qi, qt_ref, k_ref, vt_ref, *lam_refs, subln_t_ref, o_ref,
                      *scratch[:N_PROMPT_SCRATCH], overlap_work=accumulate)
    finish()


def _attention(qt, k, vt, qs, k_new, v_new, cache_k, cache_v, page_table, lq1, lk1, lq2, lk2,
               subln):
    b, _, s, _ = k.shape
    db, nt, _ = qs.shape
    n_pages = page_table.shape[1]
    tq, tk, npg = ATTN_Q_TILE, ATTN_KV_TILE, PAGES_PER_STEP
    assert s % tq == 0 and n_pages % npg == 0 and npg % SAMPLE_PARTS == 0
    assert SAMPLE_ROWS % nt == 0 and 2 * nt * N_HEADS <= SAMPLE_ROWS
    nq, n_chunks = s // tq, n_pages // npg
    assert nq % n_chunks == 0 and b * N_HEADS * (nq // n_chunks) == db
    seqs_per_head = nq // n_chunks

    def seq(bi, hi, qi):
        return (bi * N_HEADS + hi) * seqs_per_head + qi // n_chunks

    small = lambda a: pl.BlockSpec(a.shape, lambda bi, hi, qi, pt: (0, 0))
    new_rows = pl.BlockSpec((nt * N_HEADS, KEY_DIM), lambda bi, hi, qi, pt: (seq(bi, hi, qi), 0))
    per_seq = pl.BlockSpec((None, nt, Q_COLS), lambda bi, hi, qi, pt: (seq(bi, hi, qi), 0, 0))

    def page(i):
        return pl.BlockSpec(
            (None, PAGE_SIZE * N_HEADS, KEY_DIM),
            lambda bi, hi, qi, pt: (pt[seq(bi, hi, qi), (qi % n_chunks) * npg + i], 0, 0))

    subln_t = subln.reshape(VALUE_DIM, 1)
    grid_spec = pltpu.PrefetchScalarGridSpec(
        num_scalar_prefetch=1,
        grid=(b, N_HEADS, nq),
        in_specs=([pl.BlockSpec((None, None, KEY_DIM, tq), lambda bi, hi, qi, pt: (bi, hi, 0, qi)),
                   pl.BlockSpec((None, None, s, KEY_DIM), lambda bi, hi, qi, pt: (bi, hi, 0, 0)),
                   pl.BlockSpec((None, None, VALUE_DIM, s), lambda bi, hi, qi, pt: (bi, hi, 0, 0)),
                   small(lq1), small(lk1), small(lq2), small(lk2), small(subln_t),
                   per_seq, new_rows, new_rows, small(subln)]
                  + [page(i) for i in range(npg)] * 2),
        out_specs=[pl.BlockSpec((None, tq, VALUE_DIM), lambda bi, hi, qi, pt: (bi, qi, hi)),
                   per_seq],
        scratch_shapes=(
            [pltpu.VMEM((1, 2 * tq), F32),
             pltpu.VMEM((VALUE_DIM + ATTN_SUM_ROWS, 2 * tq), F32),
             pltpu.VMEM((1, 2 * tq), F32)]
            + [pltpu.VMEM((tk, 2 * tq), F32)] * 4 + [pltpu.VMEM((tk, 2 * tq), BF16)] * 4
            + [pltpu.VMEM((1, 2 * tq), F32)] * 4
            + [pltpu.VMEM((SAMPLE_ROWS, 1), F32), pltpu.VMEM((SAMPLE_ROWS, 1), F32),
               pltpu.VMEM((SAMPLE_ROWS, ATT_WIDTH), F32),
               pltpu.VMEM((npg * PAGE_SIZE, Q_COLS), BF16),
               pltpu.VMEM((npg * PAGE_SIZE, ATT_WIDTH), BF16)]),
    )
    return pl.pallas_call(
        functools.partial(_attn_kernel, n_chunks=n_chunks),
        grid_spec=grid_spec,
        out_shape=[jax.ShapeDtypeStruct((b, s, ATT_WIDTH), BF16),
                   jax.ShapeDtypeStruct((db, nt, ATT_WIDTH), F32)],
        compiler_params=pltpu.CompilerParams(
            dimension_semantics=("parallel", "parallel", "arbitrary"),
            vmem_limit_bytes=V7X_VMEM_LIMIT_BYTES),
        name="attention",
    )(page_table, qt, k, vt, lq1, lk1, lq2, lk2, subln_t, qs, k_new, v_new, subln,
      *([cache_k] * npg), *([cache_v] * npg))


def _post_kernel(h_ref, a_ref, gu_ref, vs_ref, ga_ref, gb_ref, wsp_ref, bsp_ref, wpa_ref, wpb_ref,
                 wo_ref, n2_ref, w1_ref, w2_ref, nf_ref, y_ref, *, seq_per_chunk):
    tm = h_ref.shape[0]
    row = lax.broadcasted_iota(jnp.int32, (CHUNK, CHUNK), 0)
    col = lax.broadcasted_iota(jnp.int32, (CHUNK, CHUNK), 1)
    seq_len = CHUNK // seq_per_chunk
    keep = (col <= row) & (row // seq_len == col // seq_len)
    bias = bsp_ref[...]
    n_chunks = tm // CHUNK
    mixed = [[None] * SGU_GROUPS for _ in range(n_chunks)]
    for g in range(SGU_GROUPS):
        lanes = slice(g * SGU_GROUP_DIM, (g + 1) * SGU_GROUP_DIM)
        w = jnp.where(keep, wsp_ref[g], 0.0).astype(BF16)
        vg = jnp.concatenate([vs_ref[ci * CHUNK:(ci + 1) * CHUNK, lanes].astype(BF16)
                              for ci in range(n_chunks)], axis=1)
        out = jnp.dot(w, vg, preferred_element_type=F32) + bias[:, g:g + 1]
        for ci in range(n_chunks):
            mixed[ci][g] = out[:, ci * SGU_GROUP_DIM:(ci + 1) * SGU_GROUP_DIM]
    sp = jnp.concatenate([jnp.concatenate(row, axis=1) for row in mixed], axis=0)
    sgu = (gu_ref[...] * sp).astype(BF16)
    m = (ga_ref[...] * jnp.dot(a_ref[...].astype(BF16), wpa_ref[...], preferred_element_type=F32)
         + gb_ref[...] * jnp.dot(sgu, wpb_ref[...], preferred_element_type=F32))
    h = h_ref[...] + jnp.dot(m.astype(BF16), wo_ref[...], preferred_element_type=F32)
    out = _swiglu_half_step(h, n2_ref, w1_ref, w2_ref)
    y_ref[...] = _rms(out, nf_ref[...])


def _post_stage(h, a, gu, vs, ga, gb, wsp, bsp, wpa, wpb, wo, n2, w1, w2, nf, *, seq_per_chunk):
    t = h.shape[0]
    tm = TOKEN_TILE
    assert t % tm == 0 and tm % CHUNK == 0
    widths = (D_MODEL, ATT_WIDTH, SGU_WIDTH, SGU_WIDTH, D_MODEL, D_MODEL)
    consts = (wsp, bsp, wpa, wpb, wo, n2, w1, w2, nf)
    return pl.pallas_call(
        functools.partial(_post_kernel, seq_per_chunk=seq_per_chunk),
        grid=(t // tm,),
        in_specs=[_rows(tm, w) for w in widths] + [_resident(c.shape) for c in consts],
        out_specs=_rows(tm, D_MODEL),
        out_shape=jax.ShapeDtypeStruct((t, D_MODEL), F32),
        compiler_params=pltpu.CompilerParams(dimension_semantics=("parallel",),
                                             vmem_limit_bytes=V7X_VMEM_LIMIT_BYTES),
        name="post_stage",
    )(h, a, gu, vs, ga, gb, *consts)


def kernel(x_prompt, x_sample, cache_k, cache_v, page_table, norm_ffn1, ffn1_w_in, ffn1_w_out,
           norm_mix, w_in, lambda_q1, lambda_k1, lambda_q2, lambda_k2, subln, sgu_ln_g, sgu_ln_b,
           sgu_w, sgu_b, w_proj_attn, w_proj_sgu, w_out, norm_ffn2, ffn2_w_in, ffn2_w_out,
           norm_final):
    assert norm_ffn1.shape[0] == 1, "single-layer stack"
    b, s, d = x_prompt.shape
    db, nt, _ = x_sample.shape
    assert CHUNK % nt == 0 and (db * nt) % CHUNK == 0
    tail_start = ((s - 1) // CHUNK) * CHUNK

    bf = lambda w: w[0].astype(BF16)
    pre_w = (norm_ffn1, bf(ffn1_w_in), bf(ffn1_w_out), norm_mix, bf(w_in), sgu_ln_g, sgu_ln_b)
    lam_w = (lambda_q1, lambda_k1, lambda_q2, lambda_k2, subln)
    post_w = (bf(w_proj_attn), bf(w_proj_sgu), bf(w_out), norm_ffn2, bf(ffn2_w_in), bf(ffn2_w_out),
              norm_final.reshape(1, d))

    h, k, v, gu, vs, ga, gb, qh, kh, vh = _pre_stage(x_prompt.reshape(b * s, d), *pre_w,
                                                     head_major_seq=s)
    hs, ks, vsm, gus, vss, gas, gbs, qs = _pre_stage(x_sample.reshape(db * nt, d), *pre_w)
    n_pool = cache_k.shape[1]
    a, a_s = _attention(qh, kh, vh, qs.reshape(db, nt, Q_COLS), ks, vsm,
                        cache_k.reshape(n_pool, PAGE_SIZE * N_HEADS, KEY_DIM),
                        cache_v.reshape(n_pool, PAGE_SIZE * N_HEADS, VALUE_DIM), page_table, *lam_w)

    y_prompt = _post_stage(h, a.reshape(b * s, ATT_WIDTH), gu, vs, ga, gb,
                           sgu_w[0], sgu_b[0].T, *post_w, seq_per_chunk=1)
    k_prompt = k.reshape(1, b, s, N_HEADS, KEY_DIM)
    v_prompt = v.reshape(1, b, s, N_HEADS, VALUE_DIM)
    sgu_v_prompt = vs.reshape(b, s, SGU_WIDTH)[None, :, tail_start:]

    reps = CHUNK // nt
    wsp_s = jnp.tile(sgu_w[0][:, :nt, :nt], (1, reps, reps))
    bsp_s = jnp.tile(sgu_b[0][:, :nt], (1, reps)).T
    y_sample = _post_stage(hs, a_s.reshape(db * nt, ATT_WIDTH), gus, vss, gas, gbs,
                           wsp_s, bsp_s, *post_w, seq_per_chunk=reps)
    k_sample = ks.reshape(1, db, nt, N_HEADS, KEY_DIM)
    v_sample = vsm.reshape(1, db, nt, N_HEADS, VALUE_DIM)
    sgu_v_sample = vss.reshape(1, db, nt, SGU_WIDTH)

    return (y_prompt.reshape(b, s, d), y_sample.reshape(db, nt, d), k_prompt, v_prompt,
            sgu_v_prompt, k_sample, v_sample, sgu_v_sample)
```

```python
import functools
import math

import jax
import jax.numpy as jnp
from jax import lax
from jax.experimental import pallas as pl
from jax.experimental.pallas import tpu as pltpu

D_MODEL = 1024
D_FF = 2816
N_HEADS = 4
HEAD_DIM = 64
KEY_DIM = 2 * HEAD_DIM
VALUE_DIM = 2 * HEAD_DIM
Q_COLS = N_HEADS * KEY_DIM
ATT_WIDTH = N_HEADS * VALUE_DIM
SGU_WIDTH = 512
SGU_GROUPS = 4
SGU_GROUP_DIM = SGU_WIDTH // SGU_GROUPS
CHUNK = 128
PAGE_SIZE = 128
SCALE = HEAD_DIM ** -0.5
RMS_EPS = 1e-6
LN_EPS = 1e-5
LAMBDA_INIT = 0.8 - 0.6 * math.exp(-0.3 * 0)
SQRT_HALF = math.sqrt(0.5)
LOG2_E = math.log2(math.e)

_Q0, _K0, _V0, _U0, _VG0, _GA0, _GB0, _END = 0, 512, 1024, 1536, 2048, 2560, 3584, 4608

BF16 = jnp.bfloat16
F32 = jnp.float32

V7X_VMEM_LIMIT_BYTES = 60 * 1024 * 1024

TOKEN_TILE = 256
ATTN_Q_TILE = 512
ATTN_KV_TILE = 256
ATTN_COL_GROUP = 256
ATTN_SUM_ROWS = 16
PAGES_PER_STEP = 16
SAMPLE_PARTS = 2
SAMPLE_ROWS = 64


def _rms(x, g, eps=RMS_EPS):
    return x * lax.rsqrt(jnp.mean(x * x, axis=-1, keepdims=True) + eps) * g


def _gelu(x):
    return 0.5 * x * (1.0 + lax.erf(x * SQRT_HALF))


def _swiglu_half_step(x, g_ref, w_in_ref, w_out_ref):
    xn = _rms(x, g_ref[...]).astype(BF16)
    gate_up = jnp.dot(xn, w_in_ref[...], preferred_element_type=F32)
    gate = gate_up[:, :D_FF]
    up = gate_up[:, D_FF:]
    act = (gate * jax.nn.sigmoid(gate) * up).astype(BF16)
    return x + 0.5 * jnp.dot(act, w_out_ref[...], preferred_element_type=F32)


def _diff_lambda(lq1_ref, lk1_ref, lq2_ref, lk2_ref):
    s1 = jnp.sum(lq1_ref[...] * lk1_ref[...], axis=-1, keepdims=True)
    s2 = jnp.sum(lq2_ref[...] * lk2_ref[...], axis=-1, keepdims=True)
    return jnp.exp(s1) - jnp.exp(s2) + LAMBDA_INIT


def _sub_norm(o1, o2, lam, subln):
    o = o1 - lam * o2
    return _rms(o, subln) * (1.0 - LAMBDA_INIT)


def _pre_kernel(x_ref, n1_ref, w1_ref, w2_ref, nm_ref, win_ref, lng_ref, lnb_ref,
                h_ref, k_ref, v_ref, gu_ref, vs_ref, ga_ref, gb_ref, *attn_refs):
    h = _swiglu_half_step(x_ref[...], n1_ref, w1_ref, w2_ref)
    h_ref[...] = h
    nrm = _rms(h, nm_ref[...]).astype(BF16)

    def proj(lo, hi):
        return jnp.dot(nrm, win_ref[:, lo:hi], preferred_element_type=F32)

    q = proj(_Q0, _K0) * (SCALE * LOG2_E)
    k = proj(_K0, _V0)
    v = proj(_V0, _U0)
    head = lambda z, hd: z[:, hd * KEY_DIM:(hd + 1) * KEY_DIM]
    tm = x_ref.shape[0]
    for hd in range(N_HEADS):
        k_ref[pl.ds(hd, tm, stride=N_HEADS), :] = head(k, hd)
        v_ref[pl.ds(hd, tm, stride=N_HEADS), :] = head(v, hd)
    if len(attn_refs) == 3:
        qt_ref, kh_ref, vt_ref = attn_refs
        for hd in range(N_HEADS):
            qt_ref[hd] = head(q, hd).T.astype(BF16)
            kh_ref[hd] = head(k, hd).astype(BF16)
            vt_ref[hd] = head(v, hd).T.astype(BF16)
    else:
        attn_refs[0][...] = q
    gu_ref[...] = _gelu(proj(_U0, _VG0))
    gv = _gelu(proj(_VG0, _GA0))
    mu = jnp.mean(gv, axis=-1, keepdims=True)
    gc = gv - mu
    vs_ref[...] = (gc * lax.rsqrt(jnp.mean(gc * gc, axis=-1, keepdims=True) + LN_EPS)
                   * lng_ref[...] + lnb_ref[...])
    ga_ref[...] = jax.nn.sigmoid(proj(_GA0, _GB0))
    gb_ref[...] = jax.nn.sigmoid(proj(_GB0, _END))


def _resident(shape):
    return pl.BlockSpec(shape, lambda *_: (0,) * len(shape), pipeline_mode=pl.Buffered(1))


def _rows(tm, width):
    return pl.BlockSpec((tm, width), lambda i: (i, 0))


def _pre_stage(x, n1, w1, w2, nm, win, lng, lnb, *, head_major_seq=None):
    t = x.shape[0]
    tm = TOKEN_TILE
    assert t % tm == 0
    cache_rows = _rows(tm * N_HEADS, KEY_DIM)
    cache_shape = jax.ShapeDtypeStruct((t * N_HEADS, KEY_DIM), F32)
    flat = lambda w: (_rows(tm, w), jax.ShapeDtypeStruct((t, w), F32))
    outs = [flat(D_MODEL), (cache_rows, cache_shape), (cache_rows, cache_shape), flat(SGU_WIDTH),
            flat(SGU_WIDTH), flat(D_MODEL), flat(D_MODEL)]
    if head_major_seq is not None:
        seq = head_major_seq
        assert seq % tm == 0 and t % seq == 0
        per = seq // tm
        rows = (pl.BlockSpec((None, N_HEADS, tm, KEY_DIM), lambda i: (i // per, 0, i % per, 0)),
                jax.ShapeDtypeStruct((t // seq, N_HEADS, seq, KEY_DIM), BF16))
        cols = (pl.BlockSpec((None, N_HEADS, KEY_DIM, tm), lambda i: (i // per, 0, 0, i % per)),
                jax.ShapeDtypeStruct((t // seq, N_HEADS, KEY_DIM, seq), BF16))
        outs += [cols, rows, cols]
    else:
        outs += [flat(Q_COLS)]
    return pl.pallas_call(
        _pre_kernel,
        grid=(t // tm,),
        in_specs=[_rows(tm, D_MODEL), _resident(n1.shape), _resident(w1.shape), _resident(w2.shape),
                  _resident(nm.shape), _resident(win.shape), _resident(lng.shape), _resident(lnb.shape)],
        out_specs=[o[0] for o in outs],
        out_shape=[o[1] for o in outs],
        compiler_params=pltpu.CompilerParams(dimension_semantics=("parallel",),
                                             vmem_limit_bytes=V7X_VMEM_LIMIT_BYTES),
        name="pre_stage",
    )(x, n1, w1, w2, nm, win, lng, lnb)


def _attn_prompt_body(qi, qt_ref, k_ref, vt_ref, lq1_ref, lk1_ref, lq2_ref, lk2_ref, subln_t_ref,
                      o_ref, m_ref, acc_ref, alpha_ref, *bufs, overlap_work=None):
    tq, tk = ATTN_Q_TILE, ATTN_KV_TILE
    assert tq == 2 * tk
    st, pt, bm = bufs[:4], bufs[4:8], bufs[8:]

    qt = qt_ref[...]
    dim = lax.broadcasted_iota(jnp.int32, qt.shape, 0)
    zero = jnp.zeros_like(qt)
    qqt = jnp.concatenate([jnp.where(dim < HEAD_DIM, qt, zero),
                           jnp.where(dim >= HEAD_DIM, qt, zero)], axis=1)

    groups = [slice(c, c + ATTN_COL_GROUP) for c in range(0, 2 * tq, ATTN_COL_GROUP)]

    def scores(blk, half, cols):
        start = pl.multiple_of(blk * tk, 2 * tk)
        s2 = jnp.dot(k_ref[pl.ds(start, 2 * tk), :], qqt[:, cols], preferred_element_type=F32)
        for i, s in enumerate((s2[:tk], s2[tk:])):
            st[2 * half + i][:, cols] = s
            bm[2 * half + i][:, cols] = jnp.max(s, axis=0, keepdims=True)

    ones_rows = jnp.ones((ATTN_SUM_ROWS, tk), BF16)

    def weighted(blk, pt_ref, cols):
        start = pl.multiple_of(blk * tk, tk)
        lhs = jnp.concatenate([vt_ref[:, pl.ds(start, tk)], ones_rows], axis=0)
        return jnp.dot(lhs, pt_ref[:, cols], preferred_element_type=F32)

    def softmax(buf, cols, key_offset=None):
        s = st[buf][:, cols]
        if key_offset is None:
            s_max = bm[buf][:, cols]
        else:
            key = lax.broadcasted_iota(jnp.int32, s.shape, 0) + key_offset
            qpos = lax.broadcasted_iota(jnp.int32, s.shape, 1) + cols.start % tq
            s = jnp.where(key <= qpos, s, -jnp.inf)
            s_max = jnp.max(s, axis=0, keepdims=True)
        m_old = m_ref[:, cols]
        m_new = jnp.maximum(m_old, s_max)
        alpha = jnp.exp2(m_old - m_new)
        p = jnp.exp2(s - m_new)
        m_ref[:, cols] = m_new
        pt[buf][:, cols] = p.astype(BF16)
        return alpha

    def fold_pending(t, half, cols, alpha_a, alpha_b):
        pv_a = weighted(t - 2, pt[2 * half], cols)
        pv_b = weighted(t - 1, pt[2 * half + 1], cols)
        acc = acc_ref[:, cols] + alpha_ref[:, cols] * pv_a + pv_b
        return (alpha_a if alpha_b is None else alpha_a * alpha_b) * acc

    def pair(t, parity, first=False):
        cur, nxt = parity, 1 - parity
        for cols in groups:
            scores(t + 2, nxt, cols)
            alpha_a = softmax(2 * cur, cols)
            alpha_b = softmax(2 * cur + 1, cols)
            if not first:
                acc_ref[:, cols] = fold_pending(t, nxt, cols, alpha_a, alpha_b)
            alpha_ref[:, cols] = alpha_b

    def diagonal(t, parity, first=False):
        cur, nxt = parity, 1 - parity
        for cols in groups:
            first_q = cols.start % tq
            sees_all_lo = first_q >= tk - 1
            sees_no_hi = first_q + ATTN_COL_GROUP <= tk
            alpha_a = softmax(2 * cur, cols, key_offset=None if sees_all_lo else 0)
            alpha_b = None if sees_no_hi else softmax(2 * cur + 1, cols, key_offset=tk)
            pv = weighted(t, pt[2 * cur], cols)
            if alpha_b is not None:
                pv = alpha_b * pv + weighted(t + 1, pt[2 * cur + 1], cols)
            if not first:
                pv = pv + fold_pending(t, nxt, cols, alpha_a, alpha_b)
            acc_ref[:, cols] = pv

    m_ref[...] = jnp.full(m_ref.shape, -jnp.inf, F32)
    acc_ref[...] = jnp.zeros(acc_ref.shape, F32)
    for cols in groups:
        scores(0, 0, cols)
    if overlap_work is not None:
        overlap_work()

    def write_tile():
        ot = acc_ref[:VALUE_DIM, :] * (1.0 / acc_ref[VALUE_DIM:VALUE_DIM + 1, :])
        lam = _diff_lambda(lq1_ref, lk1_ref, lq2_ref, lk2_ref)
        o = ot[:, :tq] - lam * ot[:, tq:]
        y = (o * lax.rsqrt(jnp.mean(o * o, axis=0, keepdims=True) + RMS_EPS) * subln_t_ref[...]
             * (1.0 - LAMBDA_INIT))
        o_ref[...] = y.T.astype(BF16)

    @pl.when(qi == 0)
    def _():
        diagonal(0, 0, first=True)
        write_tile()

    @pl.when(qi > 0)
    def _():
        pair(0, 0, first=True)

    def two_pairs(j, carry):
        u = 1 + 2 * j
        pair(2 * u, 1)
        pair(2 * u + 2, 0)
        return carry

    lax.fori_loop(0, (qi - 1) // 2, two_pairs, 0)

    @pl.when((qi > 0) & (qi % 2 == 0))
    def _():
        pair(2 * (qi - 1), 1)
        diagonal(2 * qi, 0)
        write_tile()

    @pl.when(qi % 2 == 1)
    def _():
        diagonal(2 * qi, 1)
        write_tile()


def _attn_sample_phases(c, n_chunks, q_ref, kn_ref, vn_ref, lq1_ref, lk1_ref, lq2_ref, lk2_ref,
                        subln_ref, k_pages, v_pages, o_ref, m_ref, l_ref, acc_ref, kb_ref, vb_ref):
    npg = len(k_pages)
    nt = q_ref.shape[0]
    nrow = SAMPLE_ROWS

    q = q_ref[...]
    qrows = jnp.concatenate([q] * (nrow // nt), axis=0)
    dim = lax.broadcasted_iota(jnp.int32, qrows.shape, 1)
    row = lax.broadcasted_iota(jnp.int32, qrows.shape, 0)
    live = (dim // HEAD_DIM == row // nt) & (row < 2 * nt * N_HEADS)
    qm_f32 = jnp.where(live, qrows, 0.0)
    qm = qm_f32.astype(BF16)
    contract_last = (((1,), (1,)), ((), ()))

    def start():
        @pl.when(c == 0)
        def _():
            head_rows = lambda ref, hd: ref[pl.ds(hd, nt, stride=N_HEADS), :]
            kn = jnp.concatenate([head_rows(kn_ref, hd) for hd in range(N_HEADS)], axis=1)
            vn = jnp.concatenate([head_rows(vn_ref, hd) for hd in range(N_HEADS)], axis=1)
            s = lax.dot_general(qm_f32, kn, contract_last, preferred_element_type=F32)
            qry_t = lax.broadcasted_iota(jnp.int32, s.shape, 0) % nt
            key_t = lax.broadcasted_iota(jnp.int32, s.shape, 1)
            s = jnp.where(key_t <= qry_t, s, -jnp.inf)
            m = jnp.max(s, axis=1, keepdims=True)
            p = jnp.exp2(s - m)
            m_ref[...] = m
            l_ref[...] = jnp.sum(p, axis=1, keepdims=True)
            acc_ref[...] = jnp.dot(p, vn, preferred_element_type=F32)

    pages_per_part = npg // SAMPLE_PARTS
    part_rows = [slice(i * pages_per_part * PAGE_SIZE, (i + 1) * pages_per_part * PAGE_SIZE)
                 for i in range(SAMPLE_PARTS)]

    def relayout(pages, buf_ref, part):
        for i in range(part * pages_per_part, (part + 1) * pages_per_part):
            rows = slice(i * PAGE_SIZE, (i + 1) * PAGE_SIZE)
            for hd in range(N_HEADS):
                lanes = slice(hd * KEY_DIM, (hd + 1) * KEY_DIM)
                tokens = pl.ds(hd, PAGE_SIZE, stride=N_HEADS)
                buf_ref[rows, lanes] = pages[i][tokens, :].astype(BF16)

    def accumulate():
        scores = []
        for part, rows in enumerate(part_rows):
            relayout(k_pages, kb_ref, part)
            scores.append(lax.dot_general(qm, kb_ref[rows, :], contract_last,
                                          preferred_element_type=F32))
        m_old = m_ref[...]
        m_new = m_old
        for s in scores:
            m_new = jnp.maximum(m_new, jnp.max(s, axis=1, keepdims=True))
        alpha = jnp.exp2(m_old - m_new)
        m_ref[...] = m_new
        l = alpha * l_ref[...]
        acc = alpha * acc_ref[...]
        for part, rows in enumerate(part_rows):
            relayout(v_pages, vb_ref, part)
            p = jnp.exp2(scores[part] - m_new)
            l = l + jnp.sum(p, axis=1, keepdims=True)
            acc = acc + jnp.dot(p.astype(BF16), vb_ref[rows, :], preferred_element_type=F32)
        l_ref[...] = l
        acc_ref[...] = acc

    def finish():
        @pl.when(c == n_chunks - 1)
        def _():
            o = acc_ref[...] * (1.0 / l_ref[...])
            lam = _diff_lambda(lq1_ref, lk1_ref, lq2_ref, lk2_ref)
            for h in range(N_HEADS):
                r0 = h * 2 * nt
                o1 = o[r0:r0 + nt, h * VALUE_DIM:(h + 1) * VALUE_DIM]
                o2 = o[r0 + nt:r0 + 2 * nt, h * VALUE_DIM:(h + 1) * VALUE_DIM]
                o_ref[:, h * VALUE_DIM:(h + 1) * VALUE_DIM] = _sub_norm(
                    o1, o2, lam, subln_ref[...])

    return start, accumulate, finish


N_PROMPT_SCRATCH = 3 + 12


def _attn_kernel(pt_ref, qt_ref, k_ref, vt_ref, lq1_ref, lk1_ref, lq2_ref, lk2_ref, subln_t_ref,
                 qs_ref, kn_ref, vn_ref, subln_ref, *rest, n_chunks):
    del pt_ref
    npg = PAGES_PER_STEP
    k_pages, v_pages = rest[:npg], rest[npg:2 * npg]
    o_ref, os_ref = rest[2 * npg:2 * npg + 2]
    scratch = rest[2 * npg + 2:]
    qi = pl.program_id(2)
    lam_refs = (lq1_ref, lk1_ref, lq2_ref, lk2_ref)
    start, accumulate, finish = _attn_sample_phases(
        qi % n_chunks, n_chunks, qs_ref, kn_ref, vn_ref, *lam_refs, subln_ref,
        k_pages, v_pages, os_ref, *scratch[N_PROMPT_SCRATCH:])
    start()
    _attn_prompt_body(qi, qt_ref, k_ref, vt_ref, *lam_refs, subln_t_ref, o_ref,
                      *scratch[:N_PROMPT_SCRATCH], overlap_work=accumulate)
    finish()


def _attention(qt, k, vt, qs, k_new, v_new, cache_k, cache_v, page_table, lq1, lk1, lq2, lk2,
               subln):
    b, _, s, _ = k.shape
    db, nt, _ = qs.shape
    n_pages = page_table.shape[1]
    tq, tk, npg = ATTN_Q_TILE, ATTN_KV_TILE, PAGES_PER_STEP
    assert s % tq == 0 and n_pages % npg == 0 and npg % SAMPLE_PARTS == 0
    assert SAMPLE_ROWS % nt == 0 and 2 * nt * N_HEADS <= SAMPLE_ROWS
    nq, n_chunks = s // tq, n_pages // npg
    assert nq % n_chunks == 0 and b * N_HEADS * (nq // n_chunks) == db
    seqs_per_head = nq // n_chunks

    def seq(bi, hi, qi):
        return (bi * N_HEADS + hi) * seqs_per_head + qi // n_chunks

    small = lambda a: pl.BlockSpec(a.shape, lambda bi, hi, qi, pt: (0, 0))
    new_rows = pl.BlockSpec((nt * N_HEADS, KEY_DIM), lambda bi, hi, qi, pt: (seq(bi, hi, qi), 0))
    per_seq = pl.BlockSpec((None, nt, Q_COLS), lambda bi, hi, qi, pt: (seq(bi, hi, qi), 0, 0))

    def page(i):
        return pl.BlockSpec(
            (None, PAGE_SIZE * N_HEADS, KEY_DIM),
            lambda bi, hi, qi, pt: (pt[seq(bi, hi, qi), (qi % n_chunks) * npg + i], 0, 0))

    subln_t = subln.reshape(VALUE_DIM, 1)
    grid_spec = pltpu.PrefetchScalarGridSpec(
        num_scalar_prefetch=1,
        grid=(b, N_HEADS, nq),
        in_specs=([pl.BlockSpec((None, None, KEY_DIM, tq), lambda bi, hi, qi, pt: (bi, hi, 0, qi)),
                   pl.BlockSpec((None, None, s, KEY_DIM), lambda bi, hi, qi, pt: (bi, hi, 0, 0)),
                   pl.BlockSpec((None, None, VALUE_DIM, s), lambda bi, hi, qi, pt: (bi, hi, 0, 0)),
                   small(lq1), small(lk1), small(lq2), small(lk2), small(subln_t),
                   per_seq, new_rows, new_rows, small(subln)]
                  + [page(i) for i in range(npg)] * 2),
        out_specs=[pl.BlockSpec((None, tq, VALUE_DIM), lambda bi, hi, qi, pt: (bi, qi, hi)),
                   per_seq],
        scratch_shapes=(
            [pltpu.VMEM((1, 2 * tq), F32),
             pltpu.VMEM((VALUE_DIM + ATTN_SUM_ROWS, 2 * tq), F32),
             pltpu.VMEM((1, 2 * tq), F32)]
            + [pltpu.VMEM((tk, 2 * tq), F32)] * 4 + [pltpu.VMEM((tk, 2 * tq), BF16)] * 4
            + [pltpu.VMEM((1, 2 * tq), F32)] * 4
            + [pltpu.VMEM((SAMPLE_ROWS, 1), F32), pltpu.VMEM((SAMPLE_ROWS, 1), F32),
               pltpu.VMEM((SAMPLE_ROWS, ATT_WIDTH), F32),
               pltpu.VMEM((npg * PAGE_SIZE, Q_COLS), BF16),
               pltpu.VMEM((npg * PAGE_SIZE, ATT_WIDTH), BF16)]),
    )
    return pl.pallas_call(
        functools.partial(_attn_kernel, n_chunks=n_chunks),
        grid_spec=grid_spec,
        out_shape=[jax.ShapeDtypeStruct((b, s, ATT_WIDTH), BF16),
                   jax.ShapeDtypeStruct((db, nt, ATT_WIDTH), F32)],
        compiler_params=pltpu.CompilerParams(
            dimension_semantics=("parallel", "parallel", "arbitrary"),
            vmem_limit_bytes=V7X_VMEM_LIMIT_BYTES),
        name="attention",
    )(page_table, qt, k, vt, lq1, lk1, lq2, lk2, subln_t, qs, k_new, v_new, subln,
      *([cache_k] * npg), *([cache_v] * npg))


def _post_kernel(h_ref, a_ref, gu_ref, vs_ref, ga_ref, gb_ref, wsp_ref, bsp_ref, wpa_ref, wpb_ref,
                 wo_ref, n2_ref, w1_ref, w2_ref, nf_ref, y_ref, *, seq_per_chunk):
    tm = h_ref.shape[0]
    row = lax.broadcasted_iota(jnp.int32, (CHUNK, CHUNK), 0)
    col = lax.broadcasted_iota(jnp.int32, (CHUNK, CHUNK), 1)
    seq_len = CHUNK // seq_per_chunk
    keep = (col <= row) & (row // seq_len == col // seq_len)
    bias = bsp_ref[...]
    n_chunks = tm // CHUNK
    mixed = [[None] * SGU_GROUPS for _ in range(n_chunks)]
    for g in range(SGU_GROUPS):
        lanes = slice(g * SGU_GROUP_DIM, (g + 1) * SGU_GROUP_DIM)
        w = jnp.where(keep, wsp_ref[g], 0.0).astype(BF16)
        vg = jnp.concatenate([vs_ref[ci * CHUNK:(ci + 1) * CHUNK, lanes].astype(BF16)
                              for ci in range(n_chunks)], axis=1)
        out = jnp.dot(w, vg, preferred_element_type=F32) + bias[:, g:g + 1]
        for ci in range(n_chunks):
            mixed[ci][g] = out[:, ci * SGU_GROUP_DIM:(ci + 1) * SGU_GROUP_DIM]
    sp = jnp.concatenate([jnp.concatenate(row, axis=1) for row in mixed], axis=0)
    sgu = (gu_ref[...] * sp).astype(BF16)
    m = (ga_ref[...] * jnp.dot(a_ref[...].astype(BF16), wpa_ref[...], preferred_element_type=F32)
         + gb_ref[...] * jnp.dot(sgu, wpb_ref[...], preferred_element_type=F32))
    h = h_ref[...] + jnp.dot(m.astype(BF16), wo_ref[...], preferred_element_type=F32)
    out = _swiglu_half_step(h, n2_ref, w1_ref, w2_ref)
    y_ref[...] = _rms(out, nf_ref[...])


def _post_stage(h, a, gu, vs, ga, gb, wsp, bsp, wpa, wpb, wo, n2, w1, w2, nf, *, seq_per_chunk):
    t = h.shape[0]
    tm = TOKEN_TILE
    assert t % tm == 0 and tm % CHUNK == 0
    widths = (D_MODEL, ATT_WIDTH, SGU_WIDTH, SGU_WIDTH, D_MODEL, D_MODEL)
    consts = (wsp, bsp, wpa, wpb, wo, n2, w1, w2, nf)
    return pl.pallas_call(
        functools.partial(_post_kernel, seq_per_chunk=seq_per_chunk),
        grid=(t // tm,),
        in_specs=[_rows(tm, w) for w in widths] + [_resident(c.shape) for c in consts],
        out_specs=_rows(tm, D_MODEL),
        out_shape=jax.ShapeDtypeStruct((t, D_MODEL), F32),
        compiler_params=pltpu.CompilerParams(dimension_semantics=("parallel",),
                                             vmem_limit_bytes=V7X_VMEM_LIMIT_BYTES),
        name="post_stage",
    )(h, a, gu, vs, ga, gb, *consts)


def kernel(x_prompt, x_sample, cache_k, cache_v, page_table, norm_ffn1, ffn1_w_in, ffn1_w_out,
           norm_mix, w_in, lambda_q1, lambda_k1, lambda_q2, lambda_k2, subln, sgu_ln_g, sgu_ln_b,
           sgu_w, sgu_b, w_proj_attn, w_proj_sgu, w_out, norm_ffn2, ffn2_w_in, ffn2_w_out,
           norm_final):
    assert norm_ffn1.shape[0] == 1, "single-layer stack"
    b, s, d = x_prompt.shape
    db, nt, _ = x_sample.shape
    assert CHUNK % nt == 0 and (db * nt) % CHUNK == 0
    tail_start = ((s - 1) // CHUNK) * CHUNK

    bf = lambda w: w[0].astype(BF16)
    pre_w = (norm_ffn1, bf(ffn1_w_in), bf(ffn1_w_out), norm_mix, bf(w_in), sgu_ln_g, sgu_ln_b)
    lam_w = (lambda_q1, lambda_k1, lambda_q2, lambda_k2, subln)
    post_w = (bf(w_proj_attn), bf(w_proj_sgu), bf(w_out), norm_ffn2, bf(ffn2_w_in), bf(ffn2_w_out),
              norm_final.reshape(1, d))

    h, k, v, gu, vs, ga, gb, qh, kh, vh = _pre_stage(x_prompt.reshape(b * s, d), *pre_w,
                                                     head_major_seq=s)
    hs, ks, vsm, gus, vss, gas, gbs, qs = _pre_stage(x_sample.reshape(db * nt, d), *pre_w)
    n_pool = cache_k.shape[1]
    a, a_s = _attention(qh, kh, vh, qs.reshape(db, nt, Q_COLS), ks, vsm,
                        cache_k.reshape(n_pool, PAGE_SIZE * N_HEADS, KEY_DIM),
                        cache_v.reshape(n_pool, PAGE_SIZE * N_HEADS, VALUE_DIM), page_table, *lam_w)

    y_prompt = _post_stage(h, a.reshape(b * s, ATT_WIDTH), gu, vs, ga, gb,
                           sgu_w[0], sgu_b[0].T, *post_w, seq_per_chunk=1)
    k_prompt = k.reshape(1, b, s, N_HEADS, KEY_DIM)
    v_prompt = v.reshape(1, b, s, N_HEADS, VALUE_DIM)
    sgu_v_prompt = vs.reshape(b, s, SGU_WIDTH)[None, :, tail_start:]

    reps = CHUNK // nt
    wsp_s = jnp.tile(sgu_w[0][:, :nt, :nt], (1, reps, reps))
    bsp_s = jnp.tile(sgu_b[0][:, :nt], (1, reps)).T
    y_sample = _post_stage(hs, a_s.reshape(db * nt, ATT_WIDTH), gus, vss, gas, gbs,
                           wsp_s, bsp_s, *post_w, seq_per_chunk=reps)
    k_sample = ks.reshape(1, db, nt, N_HEADS, KEY_DIM)
    v_sample = vsm.reshape(1, db, nt, N_HEADS, VALUE_DIM)
    sgu_v_sample = vss.reshape(1, db, nt, SGU_WIDTH)

    return (y_prompt.reshape(b, s, d), y_sample.reshape(db, nt, d), k_prompt, v_prompt,
            sgu_v_prompt, k_sample, v_sample, sgu_v_sample)
```

```python
import functools
import math

import jax
import jax.numpy as jnp
from jax import lax
from jax.experimental import pallas as pl
from jax.experimental.pallas import tpu as pltpu

D_MODEL = 1024
D_FF = 2816
N_HEADS = 4
HEAD_DIM = 64
KEY_DIM = 2 * HEAD_DIM
VALUE_DIM = 2 * HEAD_DIM
Q_COLS = N_HEADS * KEY_DIM
ATT_WIDTH = N_HEADS * VALUE_DIM
SGU_WIDTH = 512
SGU_GROUPS = 4
SGU_GROUP_DIM = SGU_WIDTH // SGU_GROUPS
CHUNK = 128
PAGE_SIZE = 128
SCALE = HEAD_DIM ** -0.5
RMS_EPS = 1e-6
LN_EPS = 1e-5
LAMBDA_INIT = 0.8 - 0.6 * math.exp(-0.3 * 0)
SQRT_HALF = math.sqrt(0.5)
LOG2_E = math.log2(math.e)

_Q0 = 0
_K0 = _Q0 + Q_COLS
_V0 = _K0 + Q_COLS
_U0 = _V0 + ATT_WIDTH
_VG0 = _U0 + SGU_WIDTH
_GA0 = _VG0 + SGU_WIDTH
_GB0 = _GA0 + D_MODEL
_END = _GB0 + D_MODEL

BF16 = jnp.bfloat16
F32 = jnp.float32

V7X_VMEM_LIMIT_BYTES = 60 * 1024 * 1024

TOKEN_TILE = 256
ATTN_Q_TILE = 512
ATTN_KV_TILE = 256
ATTN_COL_GROUP = 256
ATTN_SUM_ROWS = 16
PAGES_PER_STEP = 16
SAMPLE_PARTS = 2
SAMPLE_ROWS = 64


def _rms(x, g, eps=RMS_EPS):
    return x * lax.rsqrt(jnp.mean(x * x, axis=-1, keepdims=True) + eps) * g


def _gelu(x):
    return 0.5 * x * (1.0 + lax.erf(x * SQRT_HALF))


def _swiglu_half_step(x, g_ref, w_in_ref, w_out_ref):
    xn = _rms(x, g_ref[...]).astype(BF16)
    gate_up = jnp.dot(xn, w_in_ref[...], preferred_element_type=F32)
    gate = gate_up[:, :D_FF]
    up = gate_up[:, D_FF:]
    act = (gate * jax.nn.sigmoid(gate) * up).astype(BF16)
    return x + 0.5 * jnp.dot(act, w_out_ref[...], preferred_element_type=F32)


def _diff_lambda(lq1_ref, lk1_ref, lq2_ref, lk2_ref):
    s1 = jnp.sum(lq1_ref[...] * lk1_ref[...], axis=-1, keepdims=True)
    s2 = jnp.sum(lq2_ref[...] * lk2_ref[...], axis=-1, keepdims=True)
    return jnp.exp(s1) - jnp.exp(s2) + LAMBDA_INIT


def _sub_norm(o1, o2, lam, subln):
    o = o1 - lam * o2
    return _rms(o, subln) * (1.0 - LAMBDA_INIT)


def _pre_kernel(x_ref, n1_ref, w1_ref, w2_ref, nm_ref, win_ref, lng_ref, lnb_ref,
                h_ref, k_ref, v_ref, gu_ref, vs_ref, ga_ref, gb_ref, *attn_refs):
    h = _swiglu_half_step(x_ref[...], n1_ref, w1_ref, w2_ref)
    h_ref[...] = h
    nrm = _rms(h, nm_ref[...]).astype(BF16)

    def proj(lo, hi):
        return jnp.dot(nrm, win_ref[:, lo:hi], preferred_element_type=F32)

    q = proj(_Q0, _K0) * (SCALE * LOG2_E)
    k = proj(_K0, _V0)
    v = proj(_V0, _U0)
    head = lambda z, hd: z[:, hd * KEY_DIM:(hd + 1) * KEY_DIM]
    tm = x_ref.shape[0]
    for hd in range(N_HEADS):
        k_ref[pl.ds(hd, tm, stride=N_HEADS), :] = head(k, hd)
        v_ref[pl.ds(hd, tm, stride=N_HEADS), :] = head(v, hd)
    if len(attn_refs) == 3:
        qt_ref, kh_ref, vt_ref = attn_refs
        for hd in range(N_HEADS):
            qt_ref[hd] = head(q, hd).T.astype(BF16)
            kh_ref[hd] = head(k, hd).astype(BF16)
            vt_ref[hd] = head(v, hd).T.astype(BF16)
    else:
        attn_refs[0][...] = q
    gu_ref[...] = _gelu(proj(_U0, _VG0))
    gv = _gelu(proj(_VG0, _GA0))
    mu = jnp.mean(gv, axis=-1, keepdims=True)
    gc = gv - mu
    vs_ref[...] = (gc * lax.rsqrt(jnp.mean(gc * gc, axis=-1, keepdims=True) + LN_EPS)
                   * lng_ref[...] + lnb_ref[...])
    ga_ref[...] = jax.nn.sigmoid(proj(_GA0, _GB0))
    gb_ref[...] = jax.nn.sigmoid(proj(_GB0, _END))


def _resident(shape):
    return pl.BlockSpec(shape, lambda *_: (0,) * len(shape), pipeline_mode=pl.Buffered(1))


def _rows(tm, width):
    return pl.BlockSpec((tm, width), lambda i: (i, 0))


def _pre_stage(x, n1, w1, w2, nm, win, lng, lnb, *, head_major_seq=None):
    t = x.shape[0]
    tm = TOKEN_TILE
    assert t % tm == 0
    cache_rows = _rows(tm * N_HEADS, KEY_DIM)
    cache_shape = jax.ShapeDtypeStruct((t * N_HEADS, KEY_DIM), F32)
    flat = lambda w: (_rows(tm, w), jax.ShapeDtypeStruct((t, w), F32))
    outs = [flat(D_MODEL), (cache_rows, cache_shape), (cache_rows, cache_shape), flat(SGU_WIDTH),
            flat(SGU_WIDTH), flat(D_MODEL), flat(D_MODEL)]
    if head_major_seq is not None:
        seq = head_major_seq
        assert seq % tm == 0 and t % seq == 0
        per = seq // tm
        rows = (pl.BlockSpec((None, N_HEADS, tm, KEY_DIM), lambda i: (i // per, 0, i % per, 0)),
                jax.ShapeDtypeStruct((t // seq, N_HEADS, seq, KEY_DIM), BF16))
        cols = (pl.BlockSpec((None, N_HEADS, KEY_DIM, tm), lambda i: (i // per, 0, 0, i % per)),
                jax.ShapeDtypeStruct((t // seq, N_HEADS, KEY_DIM, seq), BF16))
        outs += [cols, rows, cols]
    else:
        outs += [flat(Q_COLS)]
    return pl.pallas_call(
        _pre_kernel,
        grid=(t // tm,),
        in_specs=[_rows(tm, D_MODEL), _resident(n1.shape), _resident(w1.shape), _resident(w2.shape),
                  _resident(nm.shape), _resident(win.shape), _resident(lng.shape), _resident(lnb.shape)],
        out_specs=[o[0] for o in outs],
        out_shape=[o[1] for o in outs],
        compiler_params=pltpu.CompilerParams(dimension_semantics=("parallel",),
                                             vmem_limit_bytes=V7X_VMEM_LIMIT_BYTES),
        name="pre_stage",
    )(x, n1, w1, w2, nm, win, lng, lnb)


def _attn_prompt_body(qi, qt_ref, k_ref, vt_ref, lq1_ref, lk1_ref, lq2_ref, lk2_ref, subln_t_ref,
                      o_ref, m_ref, acc_ref, alpha_ref, *bufs, overlap_work=None):
    tq, tk = ATTN_Q_TILE, ATTN_KV_TILE
    assert tq == 2 * tk
    st, pt, bm = bufs[:4], bufs[4:8], bufs[8:]

    qt = qt_ref[...]
    dim = lax.broadcasted_iota(jnp.int32, qt.shape, 0)
    zero = jnp.zeros_like(qt)
    qqt = jnp.concatenate([jnp.where(dim < HEAD_DIM, qt, zero),
                           jnp.where(dim >= HEAD_DIM, qt, zero)], axis=1)

    groups = [slice(c, c + ATTN_COL_GROUP) for c in range(0, 2 * tq, ATTN_COL_GROUP)]

    def scores(blk, half, cols):
        start = pl.multiple_of(blk * tk, 2 * tk)
        s2 = jnp.dot(k_ref[pl.ds(start, 2 * tk), :], qqt[:, cols], preferred_element_type=F32)
        for i, s in enumerate((s2[:tk], s2[tk:])):
            st[2 * half + i][:, cols] = s
            bm[2 * half + i][:, cols] = jnp.max(s, axis=0, keepdims=True)

    ones_rows = jnp.ones((ATTN_SUM_ROWS, tk), BF16)

    def weighted(blk, pt_ref, cols):
        start = pl.multiple_of(blk * tk, tk)
        lhs = jnp.concatenate([vt_ref[:, pl.ds(start, tk)], ones_rows], axis=0)
        return jnp.dot(lhs, pt_ref[:, cols], preferred_element_type=F32)

    def softmax(buf, cols, key_offset=None):
        s = st[buf][:, cols]
        if key_offset is None:
            s_max = bm[buf][:, cols]
        else:
            key = lax.broadcasted_iota(jnp.int32, s.shape, 0) + key_offset
            qpos = lax.broadcasted_iota(jnp.int32, s.shape, 1) + cols.start % tq
            s = jnp.where(key <= qpos, s, -jnp.inf)
            s_max = jnp.max(s, axis=0, keepdims=True)
        m_old = m_ref[:, cols]
        m_new = jnp.maximum(m_old, s_max)
        alpha = jnp.exp2(m_old - m_new)
        p = jnp.exp2(s - m_new)
        m_ref[:, cols] = m_new
        pt[buf][:, cols] = p.astype(BF16)
        return alpha

    def fold_pending(t, half, cols, alpha_a, alpha_b):
        pv_a = weighted(t - 2, pt[2 * half], cols)
        pv_b = weighted(t - 1, pt[2 * half + 1], cols)
        acc = acc_ref[:, cols] + alpha_ref[:, cols] * pv_a + pv_b
        return (alpha_a if alpha_b is None else alpha_a * alpha_b) * acc

    def pair(t, parity, first=False):
        cur, nxt = parity, 1 - parity
        for cols in groups:
            scores(t + 2, nxt, cols)
            alpha_a = softmax(2 * cur, cols)
            alpha_b = softmax(2 * cur + 1, cols)
            if not first:
                acc_ref[:, cols] = fold_pending(t, nxt, cols, alpha_a, alpha_b)
            alpha_ref[:, cols] = alpha_b

    def diagonal(t, parity, first=False):
        cur, nxt = parity, 1 - parity
        for cols in groups:
            first_q = cols.start % tq
            sees_all_lo = first_q >= tk - 1
            sees_no_hi = first_q + ATTN_COL_GROUP <= tk
            alpha_a = softmax(2 * cur, cols, key_offset=None if sees_all_lo else 0)
            alpha_b = None if sees_no_hi else softmax(2 * cur + 1, cols, key_offset=tk)
            pv = weighted(t, pt[2 * cur], cols)
            if alpha_b is not None:
                pv = alpha_b * pv + weighted(t + 1, pt[2 * cur + 1], cols)
            if not first:
                pv = pv + fold_pending(t, nxt, cols, alpha_a, alpha_b)
            acc_ref[:, cols] = pv

    m_ref[...] = jnp.full(m_ref.shape, -jnp.inf, F32)
    acc_ref[...] = jnp.zeros(acc_ref.shape, F32)
    for cols in groups:
        scores(0, 0, cols)
    if overlap_work is not None:
        overlap_work()

    def write_tile():
        ot = acc_ref[:VALUE_DIM, :] * (1.0 / acc_ref[VALUE_DIM:VALUE_DIM + 1, :])
        lam = _diff_lambda(lq1_ref, lk1_ref, lq2_ref, lk2_ref)
        o = ot[:, :tq] - lam * ot[:, tq:]
        y = (o * lax.rsqrt(jnp.mean(o * o, axis=0, keepdims=True) + RMS_EPS) * subln_t_ref[...]
             * (1.0 - LAMBDA_INIT))
        o_ref[...] = y.T.astype(BF16)

    @pl.when(qi == 0)
    def _():
        diagonal(0, 0, first=True)
        write_tile()

    @pl.when(qi > 0)
    def _():
        pair(0, 0, first=True)

    def two_pairs(j, carry):
        u = 1 + 2 * j
        pair(2 * u, 1)
        pair(2 * u + 2, 0)
        return carry

    lax.fori_loop(0, (qi - 1) // 2, two_pairs, 0)

    @pl.when((qi > 0) & (qi % 2 == 0))
    def _():
        pair(2 * (qi - 1), 1)
        diagonal(2 * qi, 0)
        write_tile()

    @pl.when(qi % 2 == 1)
    def _():
        diagonal(2 * qi, 1)
        write_tile()


def _attn_sample_phases(c, n_chunks, q_ref, kn_ref, vn_ref, lq1_ref, lk1_ref, lq2_ref, lk2_ref,
                        subln_ref, k_pages, v_pages, o_ref, m_ref, l_ref, acc_ref, kb_ref, vb_ref):
    npg = len(k_pages)
    nt = q_ref.shape[0]
    nrow = SAMPLE_ROWS

    q = q_ref[...]
    qrows = jnp.concatenate([q] * (nrow // nt), axis=0)
    dim = lax.broadcasted_iota(jnp.int32, qrows.shape, 1)
    row = lax.broadcasted_iota(jnp.int32, qrows.shape, 0)
    live = (dim // HEAD_DIM == row // nt) & (row < 2 * nt * N_HEADS)
    qm_f32 = jnp.where(live, qrows, 0.0)
    qm = qm_f32.astype(BF16)
    contract_last = (((1,), (1,)), ((), ()))

    def start():
        @pl.when(c == 0)
        def _():
            head_rows = lambda ref, hd: ref[pl.ds(hd, nt, stride=N_HEADS), :]
            kn = jnp.concatenate([head_rows(kn_ref, hd) for hd in range(N_HEADS)], axis=1)
            vn = jnp.concatenate([head_rows(vn_ref, hd) for hd in range(N_HEADS)], axis=1)
            s = lax.dot_general(qm_f32, kn, contract_last, preferred_element_type=F32)
            qry_t = lax.broadcasted_iota(jnp.int32, s.shape, 0) % nt
            key_t = lax.broadcasted_iota(jnp.int32, s.shape, 1)
            s = jnp.where(key_t <= qry_t, s, -jnp.inf)
            m = jnp.max(s, axis=1, keepdims=True)
            p = jnp.exp2(s - m)
            m_ref[...] = m
            l_ref[...] = jnp.sum(p, axis=1, keepdims=True)
            acc_ref[...] = jnp.dot(p, vn, preferred_element_type=F32)

    pages_per_part = npg // SAMPLE_PARTS
    part_rows = [slice(i * pages_per_part * PAGE_SIZE, (i + 1) * pages_per_part * PAGE_SIZE)
                 for i in range(SAMPLE_PARTS)]

    def relayout(pages, buf_ref, part):
        for i in range(part * pages_per_part, (part + 1) * pages_per_part):
            rows = slice(i * PAGE_SIZE, (i + 1) * PAGE_SIZE)
            for hd in range(N_HEADS):
                lanes = slice(hd * KEY_DIM, (hd + 1) * KEY_DIM)
                tokens = pl.ds(hd, PAGE_SIZE, stride=N_HEADS)
                buf_ref[rows, lanes] = pages[i][tokens, :].astype(BF16)

    def accumulate():
        def qk(part):
            relayout(k_pages, kb_ref, part)
            return lax.dot_general(qm, kb_ref[part_rows[part], :], contract_last,
                                   preferred_element_type=F32)

        m, l, acc = m_ref[...], l_ref[...], acc_ref[...]
        s_next = qk(0)
        for part, rows in enumerate(part_rows):
            s = s_next
            if part + 1 < SAMPLE_PARTS:
                s_next = qk(part + 1)
            relayout(v_pages, vb_ref, part)
            m_new = jnp.maximum(m, jnp.max(s, axis=1, keepdims=True))
            alpha = jnp.exp2(m - m_new)
            p = jnp.exp2(s - m_new)
            l = alpha * l + jnp.sum(p, axis=1, keepdims=True)
            acc = alpha * acc + jnp.dot(p.astype(BF16), vb_ref[rows, :],
                                        preferred_element_type=F32)
            m = m_new
        m_ref[...] = m
        l_ref[...] = l
        acc_ref[...] = acc

    def finish():
        @pl.when(c == n_chunks - 1)
        def _():
            o = acc_ref[...] * (1.0 / l_ref[...])
            lam = _diff_lambda(lq1_ref, lk1_ref, lq2_ref, lk2_ref)
            for h in range(N_HEADS):
                r0 = h * 2 * nt
                o1 = o[r0:r0 + nt, h * VALUE_DIM:(h + 1) * VALUE_DIM]
                o2 = o[r0 + nt:r0 + 2 * nt, h * VALUE_DIM:(h + 1) * VALUE_DIM]
                o_ref[:, h * VALUE_DIM:(h + 1) * VALUE_DIM] = _sub_norm(
                    o1, o2, lam, subln_ref[...])

    return start, accumulate, finish


N_PROMPT_SCRATCH = 3 + 12


def _attn_kernel(pt_ref, qt_ref, k_ref, vt_ref, lq1_ref, lk1_ref, lq2_ref, lk2_ref, subln_t_ref,
                 qs_ref, kn_ref, vn_ref, subln_ref, *rest, n_chunks):
    del pt_ref
    npg = PAGES_PER_STEP
    k_pages, v_pages = rest[:npg], rest[npg:2 * npg]
    o_ref, os_ref = rest[2 * npg:2 * npg + 2]
    scratch = rest[2 * npg + 2:]
    qi = pl.program_id(2)
    lam_refs = (lq1_ref, lk1_ref, lq2_ref, lk2_ref)
    start, accumulate, finish = _attn_sample_phases(
        qi % n_chunks, n_chunks, qs_ref, kn_ref, vn_ref, *lam_refs, subln_ref,
        k_pages, v_pages, os_ref, *scratch[N_PROMPT_SCRATCH:])
    start()
    _attn_prompt_body(qi, qt_ref, k_ref, vt_ref, *lam_refs, subln_t_ref, o_ref,
                      *scratch[:N_PROMPT_SCRATCH], overlap_work=accumulate)
    finish()


def _attention(qt, k, vt, qs, k_new, v_new, cache_k, cache_v, page_table, lq1, lk1, lq2, lk2,
               subln):
    b, _, s, _ = k.shape
    db, nt, _ = qs.shape
    n_pages = page_table.shape[1]
    tq, tk, npg = ATTN_Q_TILE, ATTN_KV_TILE, PAGES_PER_STEP
    assert s % tq == 0 and n_pages % npg == 0 and npg % SAMPLE_PARTS == 0
    assert SAMPLE_ROWS % nt == 0 and 2 * nt * N_HEADS <= SAMPLE_ROWS
    nq, n_chunks = s // tq, n_pages // npg
    assert nq % n_chunks == 0 and b * N_HEADS * (nq // n_chunks) == db
    seqs_per_head = nq // n_chunks

    def seq(bi, hi, qi):
        return (bi * N_HEADS + hi) * seqs_per_head + qi // n_chunks

    small = lambda a: pl.BlockSpec(a.shape, lambda bi, hi, qi, pt: (0, 0))
    new_rows = pl.BlockSpec((nt * N_HEADS, KEY_DIM), lambda bi, hi, qi, pt: (seq(bi, hi, qi), 0))
    per_seq = pl.BlockSpec((None, nt, Q_COLS), lambda bi, hi, qi, pt: (seq(bi, hi, qi), 0, 0))

    def page(i):
        return pl.BlockSpec(
            (None, PAGE_SIZE * N_HEADS, KEY_DIM),
            lambda bi, hi, qi, pt: (pt[seq(bi, hi, qi), (qi % n_chunks) * npg + i], 0, 0))

    subln_t = subln.reshape(VALUE_DIM, 1)
    grid_spec = pltpu.PrefetchScalarGridSpec(
        num_scalar_prefetch=1,
        grid=(b, N_HEADS, nq),
        in_specs=([pl.BlockSpec((None, None, KEY_DIM, tq), lambda bi, hi, qi, pt: (bi, hi, 0, qi)),
                   pl.BlockSpec((None, None, s, KEY_DIM), lambda bi, hi, qi, pt: (bi, hi, 0, 0)),
                   pl.BlockSpec((None, None, VALUE_DIM, s), lambda bi, hi, qi, pt: (bi, hi, 0, 0)),
                   small(lq1), small(lk1), small(lq2), small(lk2), small(subln_t),
                   per_seq, new_rows, new_rows, small(subln)]
                  + [page(i) for i in range(npg)] * 2),
        out_specs=[pl.BlockSpec((None, tq, VALUE_DIM), lambda bi, hi, qi, pt: (bi, qi, hi)),
                   per_seq],
        scratch_shapes=(
            [pltpu.VMEM((1, 2 * tq), F32),
             pltpu.VMEM((VALUE_DIM + ATTN_SUM_ROWS, 2 * tq), F32),
             pltpu.VMEM((1, 2 * tq), F32)]
            + [pltpu.VMEM((tk, 2 * tq), F32)] * 4 + [pltpu.VMEM((tk, 2 * tq), BF16)] * 4
            + [pltpu.VMEM((1, 2 * tq), F32)] * 4
            + [pltpu.VMEM((SAMPLE_ROWS, 1), F32), pltpu.VMEM((SAMPLE_ROWS, 1), F32),
               pltpu.VMEM((SAMPLE_ROWS, ATT_WIDTH), F32),
               pltpu.VMEM((npg * PAGE_SIZE, Q_COLS), BF16),
               pltpu.VMEM((npg * PAGE_SIZE, ATT_WIDTH), BF16)]),
    )
    return pl.pallas_call(
        functools.partial(_attn_kernel, n_chunks=n_chunks),
        grid_spec=grid_spec,
        out_shape=[jax.ShapeDtypeStruct((b, s, ATT_WIDTH), BF16),
                   jax.ShapeDtypeStruct((db, nt, ATT_WIDTH), F32)],
        compiler_params=pltpu.CompilerParams(
            dimension_semantics=("parallel", "parallel", "arbitrary"),
            vmem_limit_bytes=V7X_VMEM_LIMIT_BYTES),
        name="attention",
    )(page_table, qt, k, vt, lq1, lk1, lq2, lk2, subln_t, qs, k_new, v_new, subln,
      *([cache_k] * npg), *([cache_v] * npg))


def _post_kernel(h_ref, a_ref, gu_ref, vs_ref, ga_ref, gb_ref, wsp_ref, bsp_ref, wpa_ref, wpb_ref,
                 wo_ref, n2_ref, w1_ref, w2_ref, nf_ref, y_ref, *, seq_per_chunk):
    tm = h_ref.shape[0]
    row = lax.broadcasted_iota(jnp.int32, (CHUNK, CHUNK), 0)
    col = lax.broadcasted_iota(jnp.int32, (CHUNK, CHUNK), 1)
    seq_len = CHUNK // seq_per_chunk
    keep = (col <= row) & (row // seq_len == col // seq_len)
    bias = bsp_ref[...]
    n_chunks = tm // CHUNK
    mixed = [[None] * SGU_GROUPS for _ in range(n_chunks)]
    for g in range(SGU_GROUPS):
        lanes = slice(g * SGU_GROUP_DIM, (g + 1) * SGU_GROUP_DIM)
        w = jnp.where(keep, wsp_ref[g], 0.0).astype(BF16)
        vg = jnp.concatenate([vs_ref[ci * CHUNK:(ci + 1) * CHUNK, lanes].astype(BF16)
                              for ci in range(n_chunks)], axis=1)
        out = jnp.dot(w, vg, preferred_element_type=F32) + bias[:, g:g + 1]
        for ci in range(n_chunks):
            mixed[ci][g] = out[:, ci * SGU_GROUP_DIM:(ci + 1) * SGU_GROUP_DIM]
    sp = jnp.concatenate([jnp.concatenate(row, axis=1) for row in mixed], axis=0)
    sgu = (gu_ref[...] * sp).astype(BF16)
    m = (ga_ref[...] * jnp.dot(a_ref[...].astype(BF16), wpa_ref[...], preferred_element_type=F32)
         + gb_ref[...] * jnp.dot(sgu, wpb_ref[...], preferred_element_type=F32))
    h = h_ref[...] + jnp.dot(m.astype(BF16), wo_ref[...], preferred_element_type=F32)
    out = _swiglu_half_step(h, n2_ref, w1_ref, w2_ref)
    y_ref[...] = _rms(out, nf_ref[...])


def _post_stage(h, a, gu, vs, ga, gb, wsp, bsp, wpa, wpb, wo, n2, w1, w2, nf, *, seq_per_chunk):
    t = h.shape[0]
    tm = TOKEN_TILE
    assert t % tm == 0 and tm % CHUNK == 0
    widths = (D_MODEL, ATT_WIDTH, SGU_WIDTH, SGU_WIDTH, D_MODEL, D_MODEL)
    consts = (wsp, bsp, wpa, wpb, wo, n2, w1, w2, nf)
    return pl.pallas_call(
        functools.partial(_post_kernel, seq_per_chunk=seq_per_chunk),
        grid=(t // tm,),
        in_specs=[_rows(tm, w) for w in widths] + [_resident(c.shape) for c in consts],
        out_specs=_rows(tm, D_MODEL),
        out_shape=jax.ShapeDtypeStruct((t, D_MODEL), F32),
        compiler_params=pltpu.CompilerParams(dimension_semantics=("parallel",),
                                             vmem_limit_bytes=V7X_VMEM_LIMIT_BYTES),
        name="post_stage",
    )(h, a, gu, vs, ga, gb, *consts)


def kernel(x_prompt, x_sample, cache_k, cache_v, page_table, norm_ffn1, ffn1_w_in, ffn1_w_out,
           norm_mix, w_in, lambda_q1, lambda_k1, lambda_q2, lambda_k2, subln, sgu_ln_g, sgu_ln_b,
           sgu_w, sgu_b, w_proj_attn, w_proj_sgu, w_out, norm_ffn2, ffn2_w_in, ffn2_w_out,
           norm_final):
    assert norm_ffn1.shape[0] == 1, "single-layer stack"
    b, s, d = x_prompt.shape
    db, nt, _ = x_sample.shape
    assert CHUNK % nt == 0 and (db * nt) % CHUNK == 0
    tail_start = ((s - 1) // CHUNK) * CHUNK

    bf = lambda w: w[0].astype(BF16)
    pre_w = (norm_ffn1, bf(ffn1_w_in), bf(ffn1_w_out), norm_mix, bf(w_in), sgu_ln_g, sgu_ln_b)
    lam_w = (lambda_q1, lambda_k1, lambda_q2, lambda_k2, subln)
    post_w = (bf(w_proj_attn), bf(w_proj_sgu), bf(w_out), norm_ffn2, bf(ffn2_w_in), bf(ffn2_w_out),
              norm_final.reshape(1, d))

    h, k, v, gu, vs, ga, gb, qh, kh, vh = _pre_stage(x_prompt.reshape(b * s, d), *pre_w,
                                                     head_major_seq=s)
    hs, ks, vsm, gus, vss, gas, gbs, qs = _pre_stage(x_sample.reshape(db * nt, d), *pre_w)
    n_pool = cache_k.shape[1]
    a, a_s = _attention(qh, kh, vh, qs.reshape(db, nt, Q_COLS), ks, vsm,
                        cache_k.reshape(n_pool, PAGE_SIZE * N_HEADS, KEY_DIM),
                        cache_v.reshape(n_pool, PAGE_SIZE * N_HEADS, VALUE_DIM), page_table, *lam_w)

    y_prompt = _post_stage(h, a.reshape(b * s, ATT_WIDTH), gu, vs, ga, gb,
                           sgu_w[0], sgu_b[0].T, *post_w, seq_per_chunk=1)
    k_prompt = k.reshape(1, b, s, N_HEADS, KEY_DIM)
    v_prompt = v.reshape(1, b, s, N_HEADS, VALUE_DIM)
    sgu_v_prompt = vs.reshape(b, s, SGU_WIDTH)[None, :, tail_start:]

    reps = CHUNK // nt
    wsp_s = jnp.tile(sgu_w[0][:, :nt, :nt], (1, reps, reps))
    bsp_s = jnp.tile(sgu_b[0][:, :nt], (1, reps)).T
    y_sample = _post_stage(hs, a_s.reshape(db * nt, ATT_WIDTH), gus, vss, gas, gbs,
                           wsp_s, bsp_s, *post_w, seq_per_chunk=reps)
    k_sample = ks.reshape(1, db, nt, N_HEADS, KEY_DIM)
    v_sample = vsm.reshape(1, db, nt, N_HEADS, VALUE_DIM)
    sgu_v_sample = vss.reshape(1, db, nt, SGU_WIDTH)

    return (y_prompt.reshape(b, s, d), y_sample.reshape(db, nt, d), k_prompt, v_prompt,
            sgu_v_prompt, k_sample, v_sample, sgu_v_sample)
```

```python
import functools
import math

import jax
import jax.numpy as jnp
from jax import lax
from jax.experimental import pallas as pl
from jax.experimental.pallas import tpu as pltpu

D_MODEL = 1024
D_FF = 2816
N_HEADS = 4
HEAD_DIM = 64
KEY_DIM = 2 * HEAD_DIM
VALUE_DIM = 2 * HEAD_DIM
Q_COLS = N_HEADS * KEY_DIM
ATT_WIDTH = N_HEADS * VALUE_DIM
SGU_WIDTH = 512
SGU_GROUPS = 4
SGU_GROUP_DIM = SGU_WIDTH // SGU_GROUPS
CHUNK = 128
PAGE_SIZE = 128
SCALE = HEAD_DIM ** -0.5
RMS_EPS = 1e-6
LN_EPS = 1e-5
LAMBDA_INIT = 0.8 - 0.6 * math.exp(-0.3 * 0)
SQRT_HALF = math.sqrt(0.5)
LOG2_E = math.log2(math.e)

_Q0 = 0
_K0 = _Q0 + Q_COLS
_V0 = _K0 + Q_COLS
_U0 = _V0 + ATT_WIDTH
_VG0 = _U0 + SGU_WIDTH
_GA0 = _VG0 + SGU_WIDTH
_GB0 = _GA0 + D_MODEL
_END = _GB0 + D_MODEL

BF16 = jnp.bfloat16
F32 = jnp.float32

V7X_VMEM_LIMIT_BYTES = 60 * 1024 * 1024

TOKEN_TILE = 256
ATTN_Q_TILE = 512
ATTN_KV_TILE = 256
ATTN_COL_GROUP = 256
ATTN_SUM_ROWS = 16
PAGES_PER_STEP = 16
SAMPLE_PARTS = 2
SAMPLE_ROWS = 64


def _rms(x, g, eps=RMS_EPS):
    return x * lax.rsqrt(jnp.mean(x * x, axis=-1, keepdims=True) + eps) * g


def _gelu(x):
    return 0.5 * x * (1.0 + lax.erf(x * SQRT_HALF))


def _swiglu_half_step(x, g_ref, w_in_ref, w_out_ref):
    xn = _rms(x, g_ref[...]).astype(BF16)
    gate_up = jnp.dot(xn, w_in_ref[...], preferred_element_type=F32)
    gate = gate_up[:, :D_FF]
    up = gate_up[:, D_FF:]
    act = (gate * jax.nn.sigmoid(gate) * up).astype(BF16)
    return x + 0.5 * jnp.dot(act, w_out_ref[...], preferred_element_type=F32)


def _diff_lambda(lq1_ref, lk1_ref, lq2_ref, lk2_ref):
    s1 = jnp.sum(lq1_ref[...] * lk1_ref[...], axis=-1, keepdims=True)
    s2 = jnp.sum(lq2_ref[...] * lk2_ref[...], axis=-1, keepdims=True)
    return jnp.exp(s1) - jnp.exp(s2) + LAMBDA_INIT


def _sub_norm(o1, o2, lam, subln):
    o = o1 - lam * o2
    return _rms(o, subln) * (1.0 - LAMBDA_INIT)


def _pre_kernel(x_ref, n1_ref, w1_ref, w2_ref, nm_ref, win_ref, lng_ref, lnb_ref,
                h_ref, k_ref, v_ref, gu_ref, vs_ref, ga_ref, gb_ref, *attn_refs):
    h = _swiglu_half_step(x_ref[...], n1_ref, w1_ref, w2_ref)
    h_ref[...] = h
    nrm = _rms(h, nm_ref[...]).astype(BF16)

    def proj(lo, hi):
        return jnp.dot(nrm, win_ref[:, lo:hi], preferred_element_type=F32)

    q = proj(_Q0, _K0) * (SCALE * LOG2_E)
    k = proj(_K0, _V0)
    v = proj(_V0, _U0)
    head = lambda z, hd: z[:, hd * KEY_DIM:(hd + 1) * KEY_DIM]
    tm = x_ref.shape[0]
    for hd in range(N_HEADS):
        k_ref[pl.ds(hd, tm, stride=N_HEADS), :] = head(k, hd)
        v_ref[pl.ds(hd, tm, stride=N_HEADS), :] = head(v, hd)
    if len(attn_refs) == 3:
        qt_ref, kh_ref, vt_ref = attn_refs
        for hd in range(N_HEADS):
            qt_ref[hd] = head(q, hd).T.astype(BF16)
            kh_ref[hd] = head(k, hd).astype(BF16)
            vt_ref[hd] = head(v, hd).T.astype(BF16)
    else:
        attn_refs[0][...] = q
    gu_ref[...] = _gelu(proj(_U0, _VG0))
    gv = _gelu(proj(_VG0, _GA0))
    mu = jnp.mean(gv, axis=-1, keepdims=True)
    gc = gv - mu
    vs_ref[...] = (gc * lax.rsqrt(jnp.mean(gc * gc, axis=-1, keepdims=True) + LN_EPS)
                   * lng_ref[...] + lnb_ref[...])
    ga_ref[...] = jax.nn.sigmoid(proj(_GA0, _GB0))
    gb_ref[...] = jax.nn.sigmoid(proj(_GB0, _END))


def _resident(shape):
    return pl.BlockSpec(shape, lambda *_: (0,) * len(shape), pipeline_mode=pl.Buffered(1))


def _rows(tm, width):
    return pl.BlockSpec((tm, width), lambda i: (i, 0))


def _pre_stage(x, n1, w1, w2, nm, win, lng, lnb, *, head_major_seq=None):
    t = x.shape[0]
    tm = TOKEN_TILE
    assert t % tm == 0
    cache_rows = _rows(tm * N_HEADS, KEY_DIM)
    cache_shape = jax.ShapeDtypeStruct((t * N_HEADS, KEY_DIM), F32)
    flat = lambda w: (_rows(tm, w), jax.ShapeDtypeStruct((t, w), F32))
    outs = [flat(D_MODEL), (cache_rows, cache_shape), (cache_rows, cache_shape), flat(SGU_WIDTH),
            flat(SGU_WIDTH), flat(D_MODEL), flat(D_MODEL)]
    if head_major_seq is not None:
        seq = head_major_seq
        assert seq % tm == 0 and t % seq == 0
        per = seq // tm
        rows = (pl.BlockSpec((None, N_HEADS, tm, KEY_DIM), lambda i: (i // per, 0, i % per, 0)),
                jax.ShapeDtypeStruct((t // seq, N_HEADS, seq, KEY_DIM), BF16))
        cols = (pl.BlockSpec((None, N_HEADS, KEY_DIM, tm), lambda i: (i // per, 0, 0, i % per)),
                jax.ShapeDtypeStruct((t // seq, N_HEADS, KEY_DIM, seq), BF16))
        outs += [cols, rows, cols]
    else:
        outs += [flat(Q_COLS)]
    return pl.pallas_call(
        _pre_kernel,
        grid=(t // tm,),
        in_specs=[_rows(tm, D_MODEL), _resident(n1.shape), _resident(w1.shape), _resident(w2.shape),
                  _resident(nm.shape), _resident(win.shape), _resident(lng.shape), _resident(lnb.shape)],
        out_specs=[o[0] for o in outs],
        out_shape=[o[1] for o in outs],
        compiler_params=pltpu.CompilerParams(dimension_semantics=("parallel",),
                                             vmem_limit_bytes=V7X_VMEM_LIMIT_BYTES),
        name="pre_stage",
    )(x, n1, w1, w2, nm, win, lng, lnb)


def _attn_prompt_body(qi, qt_ref, k_ref, vt_ref, lq1_ref, lk1_ref, lq2_ref, lk2_ref, subln_t_ref,
                      o_ref, m_ref, acc_ref, alpha_ref, *bufs, overlap_work=None):
    tq, tk = ATTN_Q_TILE, ATTN_KV_TILE
    assert tq == 2 * tk
    st, pt, bm = bufs[:4], bufs[4:8], bufs[8:]

    qt = qt_ref[...]
    dim = lax.broadcasted_iota(jnp.int32, qt.shape, 0)
    zero = jnp.zeros_like(qt)
    qqt = jnp.concatenate([jnp.where(dim < HEAD_DIM, qt, zero),
                           jnp.where(dim >= HEAD_DIM, qt, zero)], axis=1)

    groups = [slice(c, c + ATTN_COL_GROUP) for c in range(0, 2 * tq, ATTN_COL_GROUP)]

    def scores(blk, half, cols):
        start = pl.multiple_of(blk * tk, 2 * tk)
        s2 = jnp.dot(k_ref[pl.ds(start, 2 * tk), :], qqt[:, cols], preferred_element_type=F32)
        for i, s in enumerate((s2[:tk], s2[tk:])):
            st[2 * half + i][:, cols] = s
            bm[2 * half + i][:, cols] = jnp.max(s, axis=0, keepdims=True)

    ones_rows = jnp.ones((ATTN_SUM_ROWS, tk), BF16)

    def weighted(blk, pt_ref, cols):
        start = pl.multiple_of(blk * tk, tk)
        lhs = jnp.concatenate([vt_ref[:, pl.ds(start, tk)], ones_rows], axis=0)
        return jnp.dot(lhs, pt_ref[:, cols], preferred_element_type=F32)

    def softmax(buf, cols, key_offset=None):
        s = st[buf][:, cols]
        if key_offset is None:
            s_max = bm[buf][:, cols]
        else:
            key = lax.broadcasted_iota(jnp.int32, s.shape, 0) + key_offset
            qpos = lax.broadcasted_iota(jnp.int32, s.shape, 1) + cols.start % tq
            s = jnp.where(key <= qpos, s, -jnp.inf)
            s_max = jnp.max(s, axis=0, keepdims=True)
        m_old = m_ref[:, cols]
        m_new = jnp.maximum(m_old, s_max)
        alpha = jnp.exp2(m_old - m_new)
        p = jnp.exp2(s - m_new)
        m_ref[:, cols] = m_new
        pt[buf][:, cols] = p.astype(BF16)
        return alpha

    def fold_pending(t, half, cols, alpha_a, alpha_b):
        pv_a = weighted(t - 2, pt[2 * half], cols)
        pv_b = weighted(t - 1, pt[2 * half + 1], cols)
        acc = acc_ref[:, cols] + alpha_ref[:, cols] * pv_a + pv_b
        return (alpha_a if alpha_b is None else alpha_a * alpha_b) * acc

    def pair(t, parity, first=False):
        cur, nxt = parity, 1 - parity
        for cols in groups:
            scores(t + 2, nxt, cols)
            alpha_a = softmax(2 * cur, cols)
            alpha_b = softmax(2 * cur + 1, cols)
            if not first:
                acc_ref[:, cols] = fold_pending(t, nxt, cols, alpha_a, alpha_b)
            alpha_ref[:, cols] = alpha_b

    def diagonal(t, parity, first=False):
        cur, nxt = parity, 1 - parity
        for cols in groups:
            first_q = cols.start % tq
            sees_all_lo = first_q >= tk - 1
            sees_no_hi = first_q + ATTN_COL_GROUP <= tk
            alpha_a = softmax(2 * cur, cols, key_offset=None if sees_all_lo else 0)
            alpha_b = None if sees_no_hi else softmax(2 * cur + 1, cols, key_offset=tk)
            pv = weighted(t, pt[2 * cur], cols)
            if alpha_b is not None:
                pv = alpha_b * pv + weighted(t + 1, pt[2 * cur + 1], cols)
            if not first:
                pv = pv + fold_pending(t, nxt, cols, alpha_a, alpha_b)
            acc_ref[:, cols] = pv

    m_ref[...] = jnp.full(m_ref.shape, -jnp.inf, F32)
    acc_ref[...] = jnp.zeros(acc_ref.shape, F32)
    for cols in groups:
        scores(0, 0, cols)
    if overlap_work is not None:
        overlap_work()

    def write_tile():
        ot = acc_ref[:VALUE_DIM, :] * (1.0 / acc_ref[VALUE_DIM:VALUE_DIM + 1, :])
        lam = _diff_lambda(lq1_ref, lk1_ref, lq2_ref, lk2_ref)
        o = ot[:, :tq] - lam * ot[:, tq:]
        y = (o * lax.rsqrt(jnp.mean(o * o, axis=0, keepdims=True) + RMS_EPS) * subln_t_ref[...]
             * (1.0 - LAMBDA_INIT))
        o_ref[...] = y.T.astype(BF16)

    @pl.when(qi == 0)
    def _():
        diagonal(0, 0, first=True)
        write_tile()

    @pl.when(qi > 0)
    def _():
        pair(0, 0, first=True)

    def two_pairs(j, carry):
        u = 1 + 2 * j
        pair(2 * u, 1)
        pair(2 * u + 2, 0)
        return carry

    lax.fori_loop(0, (qi - 1) // 2, two_pairs, 0)

    @pl.when((qi > 0) & (qi % 2 == 0))
    def _():
        pair(2 * (qi - 1), 1)
        diagonal(2 * qi, 0)
        write_tile()

    @pl.when(qi % 2 == 1)
    def _():
        diagonal(2 * qi, 1)
        write_tile()


def _attn_sample_phases(c, n_chunks, q_ref, kn_ref, vn_ref, lq1_ref, lk1_ref, lq2_ref, lk2_ref,
                        subln_ref, k_pages, v_pages, o_ref, m_ref, l_ref, acc_ref, kb_ref, vb_ref):
    npg = len(k_pages)
    nt = q_ref.shape[0]
    nrow = SAMPLE_ROWS

    q = q_ref[...]
    qrows = jnp.concatenate([q] * (nrow // nt), axis=0)
    dim = lax.broadcasted_iota(jnp.int32, qrows.shape, 1)
    row = lax.broadcasted_iota(jnp.int32, qrows.shape, 0)
    live = (dim // HEAD_DIM == row // nt) & (row < 2 * nt * N_HEADS)
    qm_f32 = jnp.where(live, qrows, 0.0)
    qm = qm_f32.astype(BF16)
    contract_last = (((1,), (1,)), ((), ()))

    def start():
        @pl.when(c == 0)
        def _():
            head_rows = lambda ref, hd: ref[pl.ds(hd, nt, stride=N_HEADS), :]
            kn = jnp.concatenate([head_rows(kn_ref, hd) for hd in range(N_HEADS)], axis=1)
            vn = jnp.concatenate([head_rows(vn_ref, hd) for hd in range(N_HEADS)], axis=1)
            s = lax.dot_general(qm_f32, kn, contract_last, preferred_element_type=F32)
            qry_t = lax.broadcasted_iota(jnp.int32, s.shape, 0) % nt
            key_t = lax.broadcasted_iota(jnp.int32, s.shape, 1)
            s = jnp.where(key_t <= qry_t, s, -jnp.inf)
            m = jnp.max(s, axis=1, keepdims=True)
            p = jnp.exp2(s - m)
            m_ref[...] = m
            l_ref[...] = jnp.sum(p, axis=1, keepdims=True)
            acc_ref[...] = jnp.dot(p, vn, preferred_element_type=F32)

    pages_per_part = npg // SAMPLE_PARTS
    part_rows = [slice(i * pages_per_part * PAGE_SIZE, (i + 1) * pages_per_part * PAGE_SIZE)
                 for i in range(SAMPLE_PARTS)]

    def relayout(pages, buf_ref, part):
        for i in range(part * pages_per_part, (part + 1) * pages_per_part):
            rows = slice(i * PAGE_SIZE, (i + 1) * PAGE_SIZE)
            for hd in range(N_HEADS):
                lanes = slice(hd * KEY_DIM, (hd + 1) * KEY_DIM)
                tokens = pl.ds(hd, PAGE_SIZE, stride=N_HEADS)
                buf_ref[rows, lanes] = pages[i][tokens, :].astype(BF16)

    def accumulate():
        def qk(part):
            relayout(k_pages, kb_ref, part)
            return lax.dot_general(qm, kb_ref[part_rows[part], :], contract_last,
                                   preferred_element_type=F32)

        m, l, acc = m_ref[...], l_ref[...], acc_ref[...]
        s_next = qk(0)
        for part, rows in enumerate(part_rows):
            s = s_next
            if part + 1 < SAMPLE_PARTS:
                s_next = qk(part + 1)
            relayout(v_pages, vb_ref, part)
            m_new = jnp.maximum(m, jnp.max(s, axis=1, keepdims=True))
            alpha = jnp.exp2(m - m_new)
            p = jnp.exp2(s - m_new)
            l = alpha * l + jnp.sum(p, axis=1, keepdims=True)
            acc = alpha * acc + jnp.dot(p.astype(BF16), vb_ref[rows, :],
                                        preferred_element_type=F32)
            m = m_new
        m_ref[...] = m
        l_ref[...] = l
        acc_ref[...] = acc

    def finish():
        @pl.when(c == n_chunks - 1)
        def _():
            o = acc_ref[...] * (1.0 / l_ref[...])
            lam = _diff_lambda(lq1_ref, lk1_ref, lq2_ref, lk2_ref)
            for h in range(N_HEADS):
                r0 = h * 2 * nt
                o1 = o[r0:r0 + nt, h * VALUE_DIM:(h + 1) * VALUE_DIM]
                o2 = o[r0 + nt:r0 + 2 * nt, h * VALUE_DIM:(h + 1) * VALUE_DIM]
                o_ref[:, h * VALUE_DIM:(h + 1) * VALUE_DIM] = _sub_norm(
                    o1, o2, lam, subln_ref[...])

    return start, accumulate, finish


N_PROMPT_SCRATCH = 3 + 12


def _attn_kernel(pt_ref, qt_ref, k_ref, vt_ref, lq1_ref, lk1_ref, lq2_ref, lk2_ref, subln_t_ref,
                 qs_ref, kn_ref, vn_ref, subln_ref, *rest, n_chunks):
    npg = PAGES_PER_STEP
    ck_hbm, cv_hbm, o_ref, os_ref = rest[:4]
    kbuf, vbuf, sem = rest[-3:]
    scratch = rest[4:-3]
    qi = pl.program_id(2)

    step = (pl.program_id(0) * pl.num_programs(1) + pl.program_id(1)) * pl.num_programs(2) + qi
    n_steps = pl.num_programs(0) * pl.num_programs(1) * pl.num_programs(2)

    def page_copies(at_step, slot):
        copies = []
        for i in range(npg):
            page = pt_ref[at_step * npg + i]
            copies.append(pltpu.make_async_copy(ck_hbm.at[page], kbuf.at[slot, i], sem.at[0, slot]))
            copies.append(pltpu.make_async_copy(cv_hbm.at[page], vbuf.at[slot, i], sem.at[1, slot]))
        return copies

    slot = step % 2

    @pl.when(step == 0)
    def _():
        for cp in page_copies(step, slot):
            cp.start()

    @pl.when(step + 1 < n_steps)
    def _():
        for cp in page_copies(step + 1, 1 - slot):
            cp.start()

    for cp in page_copies(step, slot):
        cp.wait()
    k_pages = [kbuf.at[slot, i] for i in range(npg)]
    v_pages = [vbuf.at[slot, i] for i in range(npg)]
    lam_refs = (lq1_ref, lk1_ref, lq2_ref, lk2_ref)
    start, accumulate, finish = _attn_sample_phases(
        qi % n_chunks, n_chunks, qs_ref, kn_ref, vn_ref, *lam_refs, subln_ref,
        k_pages, v_pages, os_ref, *scratch[N_PROMPT_SCRATCH:])
    start()
    _attn_prompt_body(qi, qt_ref, k_ref, vt_ref, *lam_refs, subln_t_ref, o_ref,
                      *scratch[:N_PROMPT_SCRATCH], overlap_work=accumulate)
    finish()


def _attention(qt, k, vt, qs, k_new, v_new, cache_k, cache_v, page_table, lq1, lk1, lq2, lk2,
               subln):
    b, _, s, _ = k.shape
    db, nt, _ = qs.shape
    n_pages = page_table.shape[1]
    tq, tk, npg = ATTN_Q_TILE, ATTN_KV_TILE, PAGES_PER_STEP
    assert s % tq == 0 and n_pages % npg == 0 and npg % SAMPLE_PARTS == 0
    assert SAMPLE_ROWS % nt == 0 and 2 * nt * N_HEADS <= SAMPLE_ROWS
    nq, n_chunks = s // tq, n_pages // npg
    assert nq % n_chunks == 0 and b * N_HEADS * (nq // n_chunks) == db
    seqs_per_head = nq // n_chunks

    def seq(bi, hi, qi):
        return (bi * N_HEADS + hi) * seqs_per_head + qi // n_chunks

    small = lambda a: pl.BlockSpec(a.shape, lambda bi, hi, qi, pt: (0, 0))
    new_rows = pl.BlockSpec((nt * N_HEADS, KEY_DIM), lambda bi, hi, qi, pt: (seq(bi, hi, qi), 0))
    per_seq = pl.BlockSpec((None, nt, Q_COLS), lambda bi, hi, qi, pt: (seq(bi, hi, qi), 0, 0))

    whole_pool = pl.BlockSpec(memory_space=pl.ANY)

    subln_t = subln.reshape(VALUE_DIM, 1)
    grid_spec = pltpu.PrefetchScalarGridSpec(
        num_scalar_prefetch=1,
        grid=(b, N_HEADS, nq),
        in_specs=([pl.BlockSpec((None, None, KEY_DIM, tq), lambda bi, hi, qi, pt: (bi, hi, 0, qi)),
                   pl.BlockSpec((None, None, s, KEY_DIM), lambda bi, hi, qi, pt: (bi, hi, 0, 0)),
                   pl.BlockSpec((None, None, VALUE_DIM, s), lambda bi, hi, qi, pt: (bi, hi, 0, 0)),
                   small(lq1), small(lk1), small(lq2), small(lk2), small(subln_t),
                   per_seq, new_rows, new_rows, small(subln)]
                  + [whole_pool, whole_pool]),
        out_specs=[pl.BlockSpec((None, tq, VALUE_DIM), lambda bi, hi, qi, pt: (bi, qi, hi)),
                   per_seq],
        scratch_shapes=(
            [pltpu.VMEM((1, 2 * tq), F32),
             pltpu.VMEM((VALUE_DIM + ATTN_SUM_ROWS, 2 * tq), F32),
             pltpu.VMEM((1, 2 * tq), F32)]
            + [pltpu.VMEM((tk, 2 * tq), F32)] * 4 + [pltpu.VMEM((tk, 2 * tq), BF16)] * 4
            + [pltpu.VMEM((1, 2 * tq), F32)] * 4
            + [pltpu.VMEM((SAMPLE_ROWS, 1), F32), pltpu.VMEM((SAMPLE_ROWS, 1), F32),
               pltpu.VMEM((SAMPLE_ROWS, ATT_WIDTH), F32),
               pltpu.VMEM((npg * PAGE_SIZE, Q_COLS), BF16),
               pltpu.VMEM((npg * PAGE_SIZE, ATT_WIDTH), BF16),
               pltpu.VMEM((2, npg, PAGE_SIZE * N_HEADS, KEY_DIM), F32),
               pltpu.VMEM((2, npg, PAGE_SIZE * N_HEADS, VALUE_DIM), F32),
               pltpu.SemaphoreType.DMA((2, 2))]),
    )
    return pl.pallas_call(
        functools.partial(_attn_kernel, n_chunks=n_chunks),
        grid_spec=grid_spec,
        out_shape=[jax.ShapeDtypeStruct((b, s, ATT_WIDTH), BF16),
                   jax.ShapeDtypeStruct((db, nt, ATT_WIDTH), F32)],
        compiler_params=pltpu.CompilerParams(
            dimension_semantics=("arbitrary", "arbitrary", "arbitrary"),
            vmem_limit_bytes=V7X_VMEM_LIMIT_BYTES),
        name="attention",
    )(page_table.reshape(-1), qt, k, vt, lq1, lk1, lq2, lk2, subln_t, qs, k_new, v_new, subln,
      cache_k, cache_v)


def _post_kernel(h_ref, a_ref, gu_ref, vs_ref, ga_ref, gb_ref, wsp_ref, bsp_ref, wpa_ref, wpb_ref,
                 wo_ref, n2_ref, w1_ref, w2_ref, nf_ref, y_ref, *, seq_per_chunk):
    tm = h_ref.shape[0]
    row = lax.broadcasted_iota(jnp.int32, (CHUNK, CHUNK), 0)
    col = lax.broadcasted_iota(jnp.int32, (CHUNK, CHUNK), 1)
    seq_len = CHUNK // seq_per_chunk
    keep = (col <= row) & (row // seq_len == col // seq_len)
    bias = bsp_ref[...]
    n_chunks = tm // CHUNK
    mixed = [[None] * SGU_GROUPS for _ in range(n_chunks)]
    for g in range(SGU_GROUPS):
        lanes = slice(g * SGU_GROUP_DIM, (g + 1) * SGU_GROUP_DIM)
        w = jnp.where(keep, wsp_ref[g], 0.0).astype(BF16)
        vg = jnp.concatenate([vs_ref[ci * CHUNK:(ci + 1) * CHUNK, lanes].astype(BF16)
                              for ci in range(n_chunks)], axis=1)
        out = jnp.dot(w, vg, preferred_element_type=F32) + bias[:, g:g + 1]
        for ci in range(n_chunks):
            mixed[ci][g] = out[:, ci * SGU_GROUP_DIM:(ci + 1) * SGU_GROUP_DIM]
    sp = jnp.concatenate([jnp.concatenate(row, axis=1) for row in mixed], axis=0)
    sgu = (gu_ref[...] * sp).astype(BF16)
    m = (ga_ref[...] * jnp.dot(a_ref[...].astype(BF16), wpa_ref[...], preferred_element_type=F32)
         + gb_ref[...] * jnp.dot(sgu, wpb_ref[...], preferred_element_type=F32))
    h = h_ref[...] + jnp.dot(m.astype(BF16), wo_ref[...], preferred_element_type=F32)
    out = _swiglu_half_step(h, n2_ref, w1_ref, w2_ref)
    y_ref[...] = _rms(out, nf_ref[...])


def _post_stage(h, a, gu, vs, ga, gb, wsp, bsp, wpa, wpb, wo, n2, w1, w2, nf, *, seq_per_chunk):
    t = h.shape[0]
    tm = TOKEN_TILE
    assert t % tm == 0 and tm % CHUNK == 0
    widths = (D_MODEL, ATT_WIDTH, SGU_WIDTH, SGU_WIDTH, D_MODEL, D_MODEL)
    consts = (wsp, bsp, wpa, wpb, wo, n2, w1, w2, nf)
    return pl.pallas_call(
        functools.partial(_post_kernel, seq_per_chunk=seq_per_chunk),
        grid=(t // tm,),
        in_specs=[_rows(tm, w) for w in widths] + [_resident(c.shape) for c in consts],
        out_specs=_rows(tm, D_MODEL),
        out_shape=jax.ShapeDtypeStruct((t, D_MODEL), F32),
        compiler_params=pltpu.CompilerParams(dimension_semantics=("parallel",),
                                             vmem_limit_bytes=V7X_VMEM_LIMIT_BYTES),
        name="post_stage",
    )(h, a, gu, vs, ga, gb, *consts)


def kernel(x_prompt, x_sample, cache_k, cache_v, page_table, norm_ffn1, ffn1_w_in, ffn1_w_out,
           norm_mix, w_in, lambda_q1, lambda_k1, lambda_q2, lambda_k2, subln, sgu_ln_g, sgu_ln_b,
           sgu_w, sgu_b, w_proj_attn, w_proj_sgu, w_out, norm_ffn2, ffn2_w_in, ffn2_w_out,
           norm_final):
    assert norm_ffn1.shape[0] == 1, "single-layer stack"
    b, s, d = x_prompt.shape
    db, nt, _ = x_sample.shape
    assert CHUNK % nt == 0 and (db * nt) % CHUNK == 0
    tail_start = ((s - 1) // CHUNK) * CHUNK

    bf = lambda w: w[0].astype(BF16)
    pre_w = (norm_ffn1, bf(ffn1_w_in), bf(ffn1_w_out), norm_mix, bf(w_in), sgu_ln_g, sgu_ln_b)
    lam_w = (lambda_q1, lambda_k1, lambda_q2, lambda_k2, subln)
    post_w = (bf(w_proj_attn), bf(w_proj_sgu), bf(w_out), norm_ffn2, bf(ffn2_w_in), bf(ffn2_w_out),
              norm_final.reshape(1, d))

    h, k, v, gu, vs, ga, gb, qh, kh, vh = _pre_stage(x_prompt.reshape(b * s, d), *pre_w,
                                                     head_major_seq=s)
    hs, ks, vsm, gus, vss, gas, gbs, qs = _pre_stage(x_sample.reshape(db * nt, d), *pre_w)
    n_pool = cache_k.shape[1]
    a, a_s = _attention(qh, kh, vh, qs.reshape(db, nt, Q_COLS), ks, vsm,
                        cache_k.reshape(n_pool, PAGE_SIZE * N_HEADS, KEY_DIM),
                        cache_v.reshape(n_pool, PAGE_SIZE * N_HEADS, VALUE_DIM), page_table, *lam_w)

    y_prompt = _post_stage(h, a.reshape(b * s, ATT_WIDTH), gu, vs, ga, gb,
                           sgu_w[0], sgu_b[0].T, *post_w, seq_per_chunk=1)
    k_prompt = k.reshape(1, b, s, N_HEADS, KEY_DIM)
    v_prompt = v.reshape(1, b, s, N_HEADS, VALUE_DIM)
    sgu_v_prompt = vs.reshape(b, s, SGU_WIDTH)[None, :, tail_start:]

    reps = CHUNK // nt
    wsp_s = jnp.tile(sgu_w[0][:, :nt, :nt], (1, reps, reps))
    bsp_s = jnp.tile(sgu_b[0][:, :nt], (1, reps)).T
    y_sample = _post_stage(hs, a_s.reshape(db * nt, ATT_WIDTH), gus, vss, gas, gbs,
                           wsp_s, bsp_s, *post_w, seq_per_chunk=reps)
    k_sample = ks.reshape(1, db, nt, N_HEADS, KEY_DIM)
    v_sample = vsm.reshape(1, db, nt, N_HEADS, VALUE_DIM)
    sgu_v_sample = vss.reshape(1, db, nt, SGU_WIDTH)

    return (y_prompt.reshape(b, s, d), y_sample.reshape(db, nt, d), k_prompt, v_prompt,
            sgu_v_prompt, k_sample, v_sample, sgu_v_sample)
```

```python
import functools
import math

import jax
import jax.numpy as jnp
from jax import lax
from jax.experimental import pallas as pl
from jax.experimental.pallas import tpu as pltpu

D_MODEL = 1024
D_FF = 2816
N_HEADS = 4
HEAD_DIM = 64
KEY_DIM = 2 * HEAD_DIM
VALUE_DIM = 2 * HEAD_DIM
Q_COLS = N_HEADS * KEY_DIM
ATT_WIDTH = N_HEADS * VALUE_DIM
SGU_WIDTH = 512
SGU_GROUPS = 4
SGU_GROUP_DIM = SGU_WIDTH // SGU_GROUPS
CHUNK = 128
PAGE_SIZE = 128
SCALE = HEAD_DIM ** -0.5
RMS_EPS = 1e-6
LN_EPS = 1e-5
LAMBDA_INIT = 0.8 - 0.6 * math.exp(-0.3 * 0)
SQRT_HALF = math.sqrt(0.5)
LOG2_E = math.log2(math.e)

_Q0 = 0
_K0 = _Q0 + Q_COLS
_V0 = _K0 + Q_COLS
_U0 = _V0 + ATT_WIDTH
_VG0 = _U0 + SGU_WIDTH
_GA0 = _VG0 + SGU_WIDTH
_GB0 = _GA0 + D_MODEL
_END = _GB0 + D_MODEL

BF16 = jnp.bfloat16
F32 = jnp.float32

V7X_VMEM_LIMIT_BYTES = 60 * 1024 * 1024

TOKEN_TILE = 256
ATTN_Q_TILE = 512
ATTN_KV_TILE = 256
ATTN_COL_GROUP = 256
ATTN_SUM_ROWS = 16
PAGES_PER_STEP = 16
SAMPLE_PARTS = 2
SAMPLE_ROWS = 64


def _rms(x, g, eps=RMS_EPS):
    return x * lax.rsqrt(jnp.mean(x * x, axis=-1, keepdims=True) + eps) * g


def _gelu(x):
    return 0.5 * x * (1.0 + lax.erf(x * SQRT_HALF))


def _swiglu_half_step(x, g_ref, w_in_ref, w_out_ref):
    xn = _rms(x, g_ref[...]).astype(BF16)
    gate_up = jnp.dot(xn, w_in_ref[...], preferred_element_type=F32)
    gate = gate_up[:, :D_FF]
    up = gate_up[:, D_FF:]
    act = (gate * jax.nn.sigmoid(gate) * up).astype(BF16)
    return x + 0.5 * jnp.dot(act, w_out_ref[...], preferred_element_type=F32)


def _diff_lambda(lq1_ref, lk1_ref, lq2_ref, lk2_ref):
    s1 = jnp.sum(lq1_ref[...] * lk1_ref[...], axis=-1, keepdims=True)
    s2 = jnp.sum(lq2_ref[...] * lk2_ref[...], axis=-1, keepdims=True)
    return jnp.exp(s1) - jnp.exp(s2) + LAMBDA_INIT


def _sub_norm(o1, o2, lam, subln):
    o = o1 - lam * o2
    return _rms(o, subln) * (1.0 - LAMBDA_INIT)


def _pre_kernel(x_ref, n1_ref, w1_ref, w2_ref, nm_ref, win_ref, lng_ref, lnb_ref,
                h_ref, k_ref, v_ref, gu_ref, vs_ref, ga_ref, gb_ref, *attn_refs):
    h = _swiglu_half_step(x_ref[...], n1_ref, w1_ref, w2_ref)
    h_ref[...] = h
    nrm = _rms(h, nm_ref[...]).astype(BF16)

    def proj(lo, hi):
        return jnp.dot(nrm, win_ref[:, lo:hi], preferred_element_type=F32)

    q = proj(_Q0, _K0) * (SCALE * LOG2_E)
    k = proj(_K0, _V0)
    v = proj(_V0, _U0)
    head = lambda z, hd: z[:, hd * KEY_DIM:(hd + 1) * KEY_DIM]
    tm = x_ref.shape[0]
    for hd in range(N_HEADS):
        k_ref[pl.ds(hd, tm, stride=N_HEADS), :] = head(k, hd)
        v_ref[pl.ds(hd, tm, stride=N_HEADS), :] = head(v, hd)
    if len(attn_refs) == 3:
        qt_ref, kh_ref, vt_ref = attn_refs
        for hd in range(N_HEADS):
            qt_ref[hd] = head(q, hd).T.astype(BF16)
            kh_ref[hd] = head(k, hd).astype(BF16)
            vt_ref[hd] = head(v, hd).T.astype(BF16)
    else:
        attn_refs[0][...] = q
    gu_ref[...] = _gelu(proj(_U0, _VG0))
    gv = _gelu(proj(_VG0, _GA0))
    mu = jnp.mean(gv, axis=-1, keepdims=True)
    gc = gv - mu
    vs_ref[...] = (gc * lax.rsqrt(jnp.mean(gc * gc, axis=-1, keepdims=True) + LN_EPS)
                   * lng_ref[...] + lnb_ref[...])
    ga_ref[...] = jax.nn.sigmoid(proj(_GA0, _GB0))
    gb_ref[...] = jax.nn.sigmoid(proj(_GB0, _END))


def _resident(shape):
    return pl.BlockSpec(shape, lambda *_: (0,) * len(shape), pipeline_mode=pl.Buffered(1))


def _rows(tm, width):
    return pl.BlockSpec((tm, width), lambda i: (i, 0))


def _pre_stage(x, n1, w1, w2, nm, win, lng, lnb, *, head_major_seq=None):
    t = x.shape[0]
    tm = TOKEN_TILE
    assert t % tm == 0
    cache_rows = _rows(tm * N_HEADS, KEY_DIM)
    cache_shape = jax.ShapeDtypeStruct((t * N_HEADS, KEY_DIM), F32)
    flat = lambda w: (_rows(tm, w), jax.ShapeDtypeStruct((t, w), F32))
    outs = [flat(D_MODEL), (cache_rows, cache_shape), (cache_rows, cache_shape), flat(SGU_WIDTH),
            flat(SGU_WIDTH), flat(D_MODEL), flat(D_MODEL)]
    if head_major_seq is not None:
        seq = head_major_seq
        assert seq % tm == 0 and t % seq == 0
        per = seq // tm
        rows = (pl.BlockSpec((None, N_HEADS, tm, KEY_DIM), lambda i: (i // per, 0, i % per, 0)),
                jax.ShapeDtypeStruct((t // seq, N_HEADS, seq, KEY_DIM), BF16))
        cols = (pl.BlockSpec((None, N_HEADS, KEY_DIM, tm), lambda i: (i // per, 0, 0, i % per)),
                jax.ShapeDtypeStruct((t // seq, N_HEADS, KEY_DIM, seq), BF16))
        outs += [cols, rows, cols]
    else:
        outs += [flat(Q_COLS)]
    return pl.pallas_call(
        _pre_kernel,
        grid=(t // tm,),
        in_specs=[_rows(tm, D_MODEL), _resident(n1.shape), _resident(w1.shape), _resident(w2.shape),
                  _resident(nm.shape), _resident(win.shape), _resident(lng.shape), _resident(lnb.shape)],
        out_specs=[o[0] for o in outs],
        out_shape=[o[1] for o in outs],
        compiler_params=pltpu.CompilerParams(dimension_semantics=("parallel",),
                                             vmem_limit_bytes=V7X_VMEM_LIMIT_BYTES),
        name="pre_stage",
    )(x, n1, w1, w2, nm, win, lng, lnb)


def _attn_prompt_body(qi, qt_ref, k_ref, vt_ref, lq1_ref, lk1_ref, lq2_ref, lk2_ref, subln_t_ref,
                      o_ref, m_ref, acc_ref, alpha_ref, *bufs, overlap_work=None):
    tq, tk = ATTN_Q_TILE, ATTN_KV_TILE
    assert tq == 2 * tk
    st, pt, bm = bufs[:4], bufs[4:8], bufs[8:]

    qt = qt_ref[...]
    dim = lax.broadcasted_iota(jnp.int32, qt.shape, 0)
    zero = jnp.zeros_like(qt)
    qqt = jnp.concatenate([jnp.where(dim < HEAD_DIM, qt, zero),
                           jnp.where(dim >= HEAD_DIM, qt, zero)], axis=1)

    groups = [slice(c, c + ATTN_COL_GROUP) for c in range(0, 2 * tq, ATTN_COL_GROUP)]

    def scores(blk, half, cols):
        start = pl.multiple_of(blk * tk, 2 * tk)
        s2 = jnp.dot(k_ref[pl.ds(start, 2 * tk), :], qqt[:, cols], preferred_element_type=F32)
        for i, s in enumerate((s2[:tk], s2[tk:])):
            st[2 * half + i][:, cols] = s
            bm[2 * half + i][:, cols] = jnp.max(s, axis=0, keepdims=True)

    ones_rows = jnp.ones((ATTN_SUM_ROWS, tk), BF16)

    def weighted(blk, pt_ref, cols):
        start = pl.multiple_of(blk * tk, tk)
        lhs = jnp.concatenate([vt_ref[:, pl.ds(start, tk)], ones_rows], axis=0)
        return jnp.dot(lhs, pt_ref[:, cols], preferred_element_type=F32)

    def softmax(buf, cols, key_offset=None):
        s = st[buf][:, cols]
        if key_offset is None:
            s_max = bm[buf][:, cols]
        else:
            key = lax.broadcasted_iota(jnp.int32, s.shape, 0) + key_offset
            qpos = lax.broadcasted_iota(jnp.int32, s.shape, 1) + cols.start % tq
            s = jnp.where(key <= qpos, s, -jnp.inf)
            s_max = jnp.max(s, axis=0, keepdims=True)
        m_old = m_ref[:, cols]
        m_new = jnp.maximum(m_old, s_max)
        alpha = jnp.exp2(m_old - m_new)
        p = jnp.exp2(s - m_new)
        m_ref[:, cols] = m_new
        pt[buf][:, cols] = p.astype(BF16)
        return alpha

    def fold_pending(t, half, cols, alpha_a, alpha_b):
        pv_a = weighted(t - 2, pt[2 * half], cols)
        pv_b = weighted(t - 1, pt[2 * half + 1], cols)
        acc = acc_ref[:, cols] + alpha_ref[:, cols] * pv_a + pv_b
        return (alpha_a if alpha_b is None else alpha_a * alpha_b) * acc

    def pair(t, parity, first=False):
        cur, nxt = parity, 1 - parity
        for cols in groups:
            scores(t + 2, nxt, cols)
            alpha_a = softmax(2 * cur, cols)
            alpha_b = softmax(2 * cur + 1, cols)
            if not first:
                acc_ref[:, cols] = fold_pending(t, nxt, cols, alpha_a, alpha_b)
            alpha_ref[:, cols] = alpha_b

    def diagonal(t, parity, first=False):
        cur, nxt = parity, 1 - parity
        for cols in groups:
            first_q = cols.start % tq
            sees_all_lo = first_q >= tk - 1
            sees_no_hi = first_q + ATTN_COL_GROUP <= tk
            alpha_a = softmax(2 * cur, cols, key_offset=None if sees_all_lo else 0)
            alpha_b = None if sees_no_hi else softmax(2 * cur + 1, cols, key_offset=tk)
            pv = weighted(t, pt[2 * cur], cols)
            if alpha_b is not None:
                pv = alpha_b * pv + weighted(t + 1, pt[2 * cur + 1], cols)
            if not first:
                pv = pv + fold_pending(t, nxt, cols, alpha_a, alpha_b)
            acc_ref[:, cols] = pv

    m_ref[...] = jnp.full(m_ref.shape, -jnp.inf, F32)
    acc_ref[...] = jnp.zeros(acc_ref.shape, F32)
    for cols in groups:
        scores(0, 0, cols)
    if overlap_work is not None:
        overlap_work()

    def write_tile():
        ot = acc_ref[:VALUE_DIM, :] * (1.0 / acc_ref[VALUE_DIM:VALUE_DIM + 1, :])
        lam = _diff_lambda(lq1_ref, lk1_ref, lq2_ref, lk2_ref)
        o = ot[:, :tq] - lam * ot[:, tq:]
        y = (o * lax.rsqrt(jnp.mean(o * o, axis=0, keepdims=True) + RMS_EPS) * subln_t_ref[...]
             * (1.0 - LAMBDA_INIT))
        o_ref[...] = y.T.astype(BF16)

    @pl.when(qi == 0)
    def _():
        diagonal(0, 0, first=True)
        write_tile()

    @pl.when(qi > 0)
    def _():
        pair(0, 0, first=True)

    def two_pairs(j, carry):
        u = 1 + 2 * j
        pair(2 * u, 1)
        pair(2 * u + 2, 0)
        return carry

    lax.fori_loop(0, (qi - 1) // 2, two_pairs, 0)

    @pl.when((qi > 0) & (qi % 2 == 0))
    def _():
        pair(2 * (qi - 1), 1)
        diagonal(2 * qi, 0)
        write_tile()

    @pl.when(qi % 2 == 1)
    def _():
        diagonal(2 * qi, 1)
        write_tile()


def _attn_sample_phases(c, n_chunks, q_ref, kn_ref, vn_ref, lq1_ref, lk1_ref, lq2_ref, lk2_ref,
                        subln_ref, k_pages, v_pages, o_ref, m_ref, l_ref, acc_ref, kb_ref, vb_ref):
    npg = len(k_pages)
    nt = q_ref.shape[0]
    nrow = SAMPLE_ROWS

    q = q_ref[...]
    qrows = jnp.concatenate([q] * (nrow // nt), axis=0)
    dim = lax.broadcasted_iota(jnp.int32, qrows.shape, 1)
    row = lax.broadcasted_iota(jnp.int32, qrows.shape, 0)
    live = (dim // HEAD_DIM == row // nt) & (row < 2 * nt * N_HEADS)
    qm_f32 = jnp.where(live, qrows, 0.0)
    qm = qm_f32.astype(BF16)
    contract_last = (((1,), (1,)), ((), ()))

    def start():
        @pl.when(c == 0)
        def _():
            head_rows = lambda ref, hd: ref[pl.ds(hd, nt, stride=N_HEADS), :]
            kn = jnp.concatenate([head_rows(kn_ref, hd) for hd in range(N_HEADS)], axis=1)
            vn = jnp.concatenate([head_rows(vn_ref, hd) for hd in range(N_HEADS)], axis=1)
            s = lax.dot_general(qm_f32, kn, contract_last, preferred_element_type=F32)
            qry_t = lax.broadcasted_iota(jnp.int32, s.shape, 0) % nt
            key_t = lax.broadcasted_iota(jnp.int32, s.shape, 1)
            s = jnp.where(key_t <= qry_t, s, -jnp.inf)
            m = jnp.max(s, axis=1, keepdims=True)
            p = jnp.exp2(s - m)
            m_ref[...] = m
            l_ref[...] = jnp.sum(p, axis=1, keepdims=True)
            acc_ref[...] = jnp.dot(p, vn, preferred_element_type=F32)

    pages_per_part = npg // SAMPLE_PARTS
    part_rows = [slice(i * pages_per_part * PAGE_SIZE, (i + 1) * pages_per_part * PAGE_SIZE)
                 for i in range(SAMPLE_PARTS)]

    def relayout(pages, buf_ref, part):
        for i in range(part * pages_per_part, (part + 1) * pages_per_part):
            rows = slice(i * PAGE_SIZE, (i + 1) * PAGE_SIZE)
            for hd in range(N_HEADS):
                lanes = slice(hd * KEY_DIM, (hd + 1) * KEY_DIM)
                tokens = pl.ds(hd, PAGE_SIZE, stride=N_HEADS)
                buf_ref[rows, lanes] = pages[i][tokens, :].astype(BF16)

    def accumulate():
        def qk(part):
            relayout(k_pages, kb_ref, part)
            return lax.dot_general(qm, kb_ref[part_rows[part], :], contract_last,
                                   preferred_element_type=F32)

        m, l, acc = m_ref[...], l_ref[...], acc_ref[...]
        s_next = qk(0)
        for part, rows in enumerate(part_rows):
            s = s_next
            if part + 1 < SAMPLE_PARTS:
                s_next = qk(part + 1)
            relayout(v_pages, vb_ref, part)
            m_new = jnp.maximum(m, jnp.max(s, axis=1, keepdims=True))
            alpha = jnp.exp2(m - m_new)
            p = jnp.exp2(s - m_new)
            l = alpha * l + jnp.sum(p, axis=1, keepdims=True)
            acc = alpha * acc + jnp.dot(p.astype(BF16), vb_ref[rows, :],
                                        preferred_element_type=F32)
            m = m_new
        m_ref[...] = m
        l_ref[...] = l
        acc_ref[...] = acc

    def finish():
        @pl.when(c == n_chunks - 1)
        def _():
            o = acc_ref[...] * (1.0 / l_ref[...])
            lam = _diff_lambda(lq1_ref, lk1_ref, lq2_ref, lk2_ref)
            for h in range(N_HEADS):
                r0 = h * 2 * nt
                o1 = o[r0:r0 + nt, h * VALUE_DIM:(h + 1) * VALUE_DIM]
                o2 = o[r0 + nt:r0 + 2 * nt, h * VALUE_DIM:(h + 1) * VALUE_DIM]
                o_ref[:, h * VALUE_DIM:(h + 1) * VALUE_DIM] = _sub_norm(
                    o1, o2, lam, subln_ref[...])

    return start, accumulate, finish


N_PROMPT_SCRATCH = 3 + 12


def _attn_kernel(pt_ref, qt_ref, k_ref, vt_ref, lq1_ref, lk1_ref, lq2_ref, lk2_ref, subln_t_ref,
                 qs_ref, kn_ref, vn_ref, subln_ref, *rest, n_chunks):
    npg = PAGES_PER_STEP
    ck_hbm, cv_hbm, o_ref, os_ref = rest[:4]
    kbuf, vbuf, sem = rest[-3:]
    scratch = rest[4:-3]
    qi = pl.program_id(2)

    step = (pl.program_id(0) * pl.num_programs(1) + pl.program_id(1)) * pl.num_programs(2) + qi
    n_steps = pl.num_programs(0) * pl.num_programs(1) * pl.num_programs(2)

    def page_copies(at_step, slot):
        copies = []
        for i in range(npg):
            page = pt_ref[at_step * npg + i]
            copies.append(pltpu.make_async_copy(ck_hbm.at[page], kbuf.at[slot, i], sem.at[0, slot]))
            copies.append(pltpu.make_async_copy(cv_hbm.at[page], vbuf.at[slot, i], sem.at[1, slot]))
        return copies

    slot = step % 2

    @pl.when(step == 0)
    def _():
        for j, cp in enumerate(page_copies(step, slot)):
            cp.start(priority=j % 2)

    @pl.when(step + 1 < n_steps)
    def _():
        for j, cp in enumerate(page_copies(step + 1, 1 - slot)):
            cp.start(priority=j % 2)

    for cp in page_copies(step, slot):
        cp.wait()
    k_pages = [kbuf.at[slot, i] for i in range(npg)]
    v_pages = [vbuf.at[slot, i] for i in range(npg)]
    lam_refs = (lq1_ref, lk1_ref, lq2_ref, lk2_ref)
    start, accumulate, finish = _attn_sample_phases(
        qi % n_chunks, n_chunks, qs_ref, kn_ref, vn_ref, *lam_refs, subln_ref,
        k_pages, v_pages, os_ref, *scratch[N_PROMPT_SCRATCH:])
    start()
    _attn_prompt_body(qi, qt_ref, k_ref, vt_ref, *lam_refs, subln_t_ref, o_ref,
                      *scratch[:N_PROMPT_SCRATCH], overlap_work=accumulate)
    finish()


def _attention(qt, k, vt, qs, k_new, v_new, cache_k, cache_v, page_table, lq1, lk1, lq2, lk2,
               subln):
    b, _, s, _ = k.shape
    db, nt, _ = qs.shape
    n_pages = page_table.shape[1]
    tq, tk, npg = ATTN_Q_TILE, ATTN_KV_TILE, PAGES_PER_STEP
    assert s % tq == 0 and n_pages % npg == 0 and npg % SAMPLE_PARTS == 0
    assert SAMPLE_ROWS % nt == 0 and 2 * nt * N_HEADS <= SAMPLE_ROWS
    nq, n_chunks = s // tq, n_pages // npg
    assert nq % n_chunks == 0 and b * N_HEADS * (nq // n_chunks) == db
    seqs_per_head = nq // n_chunks

    def seq(bi, hi, qi):
        return (bi * N_HEADS + hi) * seqs_per_head + qi // n_chunks

    small = lambda a: pl.BlockSpec(a.shape, lambda bi, hi, qi, pt: (0, 0))
    new_rows = pl.BlockSpec((nt * N_HEADS, KEY_DIM), lambda bi, hi, qi, pt: (seq(bi, hi, qi), 0))
    per_seq = pl.BlockSpec((None, nt, Q_COLS), lambda bi, hi, qi, pt: (seq(bi, hi, qi), 0, 0))

    whole_pool = pl.BlockSpec(memory_space=pl.ANY)

    subln_t = subln.reshape(VALUE_DIM, 1)
    grid_spec = pltpu.PrefetchScalarGridSpec(
        num_scalar_prefetch=1,
        grid=(b, N_HEADS, nq),
        in_specs=([pl.BlockSpec((None, None, KEY_DIM, tq), lambda bi, hi, qi, pt: (bi, hi, 0, qi)),
                   pl.BlockSpec((None, None, s, KEY_DIM), lambda bi, hi, qi, pt: (bi, hi, 0, 0)),
                   pl.BlockSpec((None, None, VALUE_DIM, s), lambda bi, hi, qi, pt: (bi, hi, 0, 0)),
                   small(lq1), small(lk1), small(lq2), small(lk2), small(subln_t),
                   per_seq, new_rows, new_rows, small(subln)]
                  + [whole_pool, whole_pool]),
        out_specs=[pl.BlockSpec((None, tq, VALUE_DIM), lambda bi, hi, qi, pt: (bi, qi, hi)),
                   per_seq],
        scratch_shapes=(
            [pltpu.VMEM((1, 2 * tq), F32),
             pltpu.VMEM((VALUE_DIM + ATTN_SUM_ROWS, 2 * tq), F32),
             pltpu.VMEM((1, 2 * tq), F32)]
            + [pltpu.VMEM((tk, 2 * tq), F32)] * 4 + [pltpu.VMEM((tk, 2 * tq), BF16)] * 4
            + [pltpu.VMEM((1, 2 * tq), F32)] * 4
            + [pltpu.VMEM((SAMPLE_ROWS, 1), F32), pltpu.VMEM((SAMPLE_ROWS, 1), F32),
               pltpu.VMEM((SAMPLE_ROWS, ATT_WIDTH), F32),
               pltpu.VMEM((npg * PAGE_SIZE, Q_COLS), BF16),
               pltpu.VMEM((npg * PAGE_SIZE, ATT_WIDTH), BF16),
               pltpu.VMEM((2, npg, PAGE_SIZE * N_HEADS, KEY_DIM), F32),
               pltpu.VMEM((2, npg, PAGE_SIZE * N_HEADS, VALUE_DIM), F32),
               pltpu.SemaphoreType.DMA((2, 2))]),
    )
    return pl.pallas_call(
        functools.partial(_attn_kernel, n_chunks=n_chunks),
        grid_spec=grid_spec,
        out_shape=[jax.ShapeDtypeStruct((b, s, ATT_WIDTH), BF16),
                   jax.ShapeDtypeStruct((db, nt, ATT_WIDTH), F32)],
        compiler_params=pltpu.CompilerParams(
            dimension_semantics=("arbitrary", "arbitrary", "arbitrary"),
            vmem_limit_bytes=V7X_VMEM_LIMIT_BYTES),
        name="attention",
    )(page_table.reshape(-1), qt, k, vt, lq1, lk1, lq2, lk2, subln_t, qs, k_new, v_new, subln,
      cache_k, cache_v)


def _post_kernel(h_ref, a_ref, gu_ref, vs_ref, ga_ref, gb_ref, wsp_ref, bsp_ref, wpa_ref, wpb_ref,
                 wo_ref, n2_ref, w1_ref, w2_ref, nf_ref, y_ref, *, seq_per_chunk):
    tm = h_ref.shape[0]
    row = lax.broadcasted_iota(jnp.int32, (CHUNK, CHUNK), 0)
    col = lax.broadcasted_iota(jnp.int32, (CHUNK, CHUNK), 1)
    seq_len = CHUNK // seq_per_chunk
    keep = (col <= row) & (row // seq_len == col // seq_len)
    bias = bsp_ref[...]
    n_chunks = tm // CHUNK
    mixed = [[None] * SGU_GROUPS for _ in range(n_chunks)]
    for g in range(SGU_GROUPS):
        lanes = slice(g * SGU_GROUP_DIM, (g + 1) * SGU_GROUP_DIM)
        w = jnp.where(keep, wsp_ref[g], 0.0).astype(BF16)
        vg = jnp.concatenate([vs_ref[ci * CHUNK:(ci + 1) * CHUNK, lanes].astype(BF16)
                              for ci in range(n_chunks)], axis=1)
        out = jnp.dot(w, vg, preferred_element_type=F32) + bias[:, g:g + 1]
        for ci in range(n_chunks):
            mixed[ci][g] = out[:, ci * SGU_GROUP_DIM:(ci + 1) * SGU_GROUP_DIM]
    sp = jnp.concatenate([jnp.concatenate(row, axis=1) for row in mixed], axis=0)
    sgu = (gu_ref[...] * sp).astype(BF16)
    m = (ga_ref[...] * jnp.dot(a_ref[...].astype(BF16), wpa_ref[...], preferred_element_type=F32)
         + gb_ref[...] * jnp.dot(sgu, wpb_ref[...], preferred_element_type=F32))
    h = h_ref[...] + jnp.dot(m.astype(BF16), wo_ref[...], preferred_element_type=F32)
    out = _swiglu_half_step(h, n2_ref, w1_ref, w2_ref)
    y_ref[...] = _rms(out, nf_ref[...])


def _post_stage(h, a, gu, vs, ga, gb, wsp, bsp, wpa, wpb, wo, n2, w1, w2, nf, *, seq_per_chunk):
    t = h.shape[0]
    tm = TOKEN_TILE
    assert t % tm == 0 and tm % CHUNK == 0
    widths = (D_MODEL, ATT_WIDTH, SGU_WIDTH, SGU_WIDTH, D_MODEL, D_MODEL)
    consts = (wsp, bsp, wpa, wpb, wo, n2, w1, w2, nf)
    return pl.pallas_call(
        functools.partial(_post_kernel, seq_per_chunk=seq_per_chunk),
        grid=(t // tm,),
        in_specs=[_rows(tm, w) for w in widths] + [_resident(c.shape) for c in consts],
        out_specs=_rows(tm, D_MODEL),
        out_shape=jax.ShapeDtypeStruct((t, D_MODEL), F32),
        compiler_params=pltpu.CompilerParams(dimension_semantics=("parallel",),
                                             vmem_limit_bytes=V7X_VMEM_LIMIT_BYTES),
        name="post_stage",
    )(h, a, gu, vs, ga, gb, *consts)


def kernel(x_prompt, x_sample, cache_k, cache_v, page_table, norm_ffn1, ffn1_w_in, ffn1_w_out,
           norm_mix, w_in, lambda_q1, lambda_k1, lambda_q2, lambda_k2, subln, sgu_ln_g, sgu_ln_b,
           sgu_w, sgu_b, w_proj_attn, w_proj_sgu, w_out, norm_ffn2, ffn2_w_in, ffn2_w_out,
           norm_final):
    assert norm_ffn1.shape[0] == 1, "single-layer stack"
    b, s, d = x_prompt.shape
    db, nt, _ = x_sample.shape
    assert CHUNK % nt == 0 and (db * nt) % CHUNK == 0
    tail_start = ((s - 1) // CHUNK) * CHUNK

    bf = lambda w: w[0].astype(BF16)
    pre_w = (norm_ffn1, bf(ffn1_w_in), bf(ffn1_w_out), norm_mix, bf(w_in), sgu_ln_g, sgu_ln_b)
    lam_w = (lambda_q1, lambda_k1, lambda_q2, lambda_k2, subln)
    post_w = (bf(w_proj_attn), bf(w_proj_sgu), bf(w_out), norm_ffn2, bf(ffn2_w_in), bf(ffn2_w_out),
              norm_final.reshape(1, d))

    h, k, v, gu, vs, ga, gb, qh, kh, vh = _pre_stage(x_prompt.reshape(b * s, d), *pre_w,
                                                     head_major_seq=s)
    hs, ks, vsm, gus, vss, gas, gbs, qs = _pre_stage(x_sample.reshape(db * nt, d), *pre_w)
    n_pool = cache_k.shape[1]
    a, a_s = _attention(qh, kh, vh, qs.reshape(db, nt, Q_COLS), ks, vsm,
                        cache_k.reshape(n_pool, PAGE_SIZE * N_HEADS, KEY_DIM),
                        cache_v.reshape(n_pool, PAGE_SIZE * N_HEADS, VALUE_DIM), page_table, *lam_w)

    y_prompt = _post_stage(h, a.reshape(b * s, ATT_WIDTH), gu, vs, ga, gb,
                           sgu_w[0], sgu_b[0].T, *post_w, seq_per_chunk=1)
    k_prompt = k.reshape(1, b, s, N_HEADS, KEY_DIM)
    v_prompt = v.reshape(1, b, s, N_HEADS, VALUE_DIM)
    sgu_v_prompt = vs.reshape(b, s, SGU_WIDTH)[None, :, tail_start:]

    reps = CHUNK // nt
    wsp_s = jnp.tile(sgu_w[0][:, :nt, :nt], (1, reps, reps))
    bsp_s = jnp.tile(sgu_b[0][:, :nt], (1, reps)).T
    y_sample = _post_stage(hs, a_s.reshape(db * nt, ATT_WIDTH), gus, vss, gas, gbs,
                           wsp_s, bsp_s, *post_w, seq_per_chunk=reps)
    k_sample = ks.reshape(1, db, nt, N_HEADS, KEY_DIM)
    v_sample = vsm.reshape(1, db, nt, N_HEADS, VALUE_DIM)
    sgu_v_sample = vss.reshape(1, db, nt, SGU_WIDTH)

    return (y_prompt.reshape(b, s, d), y_sample.reshape(db, nt, d), k_prompt, v_prompt,
            sgu_v_prompt, k_sample, v_sample, sgu_v_sample)
```
